```python
import math
import jax, jax.numpy as jnp
from jax import lax
import numpy as np

D_MODEL = 4096
BATCH = 4
SEQ = 2048
DEPTH = 4
DEC_BATCH = 32
DEC_SEQ = 1
PAST_LEN = 8192
PAGE_SIZE = 128

D_MIX = D_MODEL
D_SSM = D_MIX // 2
D_ATT = D_MIX - D_SSM
SSM_GROUP = 16
N_SSM_GROUPS = D_SSM // SSM_GROUP
SSM_STATE = 64
HEAD_DIM = 128
N_HEADS = D_ATT // HEAD_DIM
N_KV_HEADS = 4
KV_GROUP = N_HEADS // N_KV_HEADS
D_KV = N_KV_HEADS * HEAD_DIM
WINDOW = 128
ROPE_THETA = 10000.0
EPS = 1e-6
BLOCK = 128
DT_MIN = 1e-3
DT_MAX = 1e-1
SPLITS = (D_SSM, 2 * D_SSM, 2 * D_SSM + D_ATT, 2 * D_SSM + D_ATT + D_KV, 2 * D_SSM + D_ATT + 2 * D_KV)
D_IN = 2 * D_SSM + 2 * D_ATT + 2 * D_KV

kernel_name = 'hymba_s5_swa_sink_adaln_step'


def rmsnorm(x, g):
    xf = x.astype(jnp.float32)
    y = xf * lax.rsqrt(jnp.mean(xf * xf, axis=-1, keepdims=True) + EPS)
    return (y * g.astype(jnp.float32)).astype(x.dtype)


def rope(x, pos):
    half = HEAD_DIM // 2
    inv = ROPE_THETA ** (-jnp.arange(half, dtype=jnp.float32) / half)
    ang = pos.astype(jnp.float32)[:, None] * inv[None, :]
    cos = jnp.cos(ang)[None, :, None, :]
    sin = jnp.sin(ang)[None, :, None, :]
    xf = x.astype(jnp.float32)
    x1, x2 = xf[..., :half], xf[..., half:]
    return jnp.concatenate([x1 * cos - x2 * sin, x2 * cos + x1 * sin], axis=-1).astype(x.dtype)


def ssm_discretise(a_re, a_im, log_dt, b_re, b_im):
    ar = a_re.astype(jnp.float32)
    ai = a_im.astype(jnp.float32)
    dt = jnp.exp(log_dt.astype(jnp.float32))[:, None]
    mag = jnp.exp(ar * dt)
    lr = mag * jnp.cos(ai * dt)
    li = mag * jnp.sin(ai * dt)
    nr = lr - 1.0
    ni = li
    den = ar * ar + ai * ai
    qr = (nr * ar + ni * ai) / den
    qi = (ni * ar - nr * ai) / den
    br = b_re.astype(jnp.float32)
    bi = b_im.astype(jnp.float32)
    bbr = qr[..., None] * br - qi[..., None] * bi
    bbi = qr[..., None] * bi + qi[..., None] * br
    return lr, li, bbr, bbi


def _complex_affine_combine(e1, e2):
    a1r, a1i, b1r, b1i = e1
    a2r, a2i, b2r, b2i = e2
    return (a2r * a1r - a2i * a1i,
            a2r * a1i + a2i * a1r,
            a2r * b1r - a2i * b1i + b2r,
            a2r * b1i + a2i * b1r + b2i)


def ssm_scan(u, h_re, h_im, lr, li, bbr, bbi, cm_re, cm_im, d_skip):
    bsz, L = u.shape[0], u.shape[1]
    blk = BLOCK if L % BLOCK == 0 else L
    nb = L // blk
    ub = jnp.swapaxes(u.reshape(bsz, nb, blk, N_SSM_GROUPS, SSM_GROUP), 0, 1)

    def step(carry, u_blk):
        hr, hi = carry
        xr = jnp.einsum('blgc,gpc->blgp', u_blk, bbr)
        xi = jnp.einsum('blgc,gpc->blgp', u_blk, bbi)
        ar = jnp.broadcast_to(lr, xr.shape)
        ai = jnp.broadcast_to(li, xi.shape)
        cr, ci, sr, si = lax.associative_scan(_complex_affine_combine, (ar, ai, xr, xi), axis=1)
        st_r = cr * hr[:, None] - ci * hi[:, None] + sr
        st_i = cr * hi[:, None] + ci * hr[:, None] + si
        y = (jnp.einsum('blgp,gcp->blgc', st_r, cm_re)
             - jnp.einsum('blgp,gcp->blgc', st_i, cm_im))
        return (st_r[:, -1], st_i[:, -1]), y

    (hr, hi), ys = lax.scan(step, (h_re, h_im), ub)
    y = jnp.swapaxes(ys, 0, 1).reshape(bsz, L, N_SSM_GROUPS, SSM_GROUP)
    y = y + d_skip.reshape(N_SSM_GROUPS, SSM_GROUP) * u
    return y, hr, hi


def window_attention(q, k_all, v_all, q_pos0, sinks):
    bsz, L = q.shape[0], q.shape[1]
    blk = BLOCK if L % BLOCK == 0 else L
    nb = L // blk
    nk = WINDOW + blk
    idx = jnp.arange(nb)[:, None] * blk + jnp.arange(nk)[None, :]
    kb = k_all[:, idx]
    vb = v_all[:, idx]
    qb = q.reshape(bsz, nb, blk, N_KV_HEADS, KV_GROUP, HEAD_DIM)
    s = jnp.einsum('bnqkgd,bnskd->bnkgqs', qb, kb,
                   preferred_element_type=jnp.float32) * (HEAD_DIM ** -0.5)
    r = jnp.arange(blk)[:, None]
    cc = jnp.arange(nk)[None, :]
    rel = cc - WINDOW - r
    kpos = q_pos0 - WINDOW + jnp.arange(nb)[:, None, None] * blk + cc[None]
    valid = ((rel > -WINDOW) & (rel <= 0))[None] & (kpos >= 0)
    s = jnp.where(valid[None, :, None, None], s, -jnp.inf)
    sink = sinks.astype(jnp.float32).reshape(N_KV_HEADS, KV_GROUP)[None, None, :, :, None, None]
    m = jnp.maximum(jnp.max(s, axis=-1, keepdims=True), sink)
    p = jnp.exp(s - m)
    den = jnp.sum(p, axis=-1, keepdims=True) + jnp.exp(sink - m)
    o = jnp.einsum('bnkgqs,bnskd->bnqkgd', (p / den).astype(vb.dtype), vb)
    return o.reshape(bsz, L, N_HEADS * HEAD_DIM)


def mixer_layer(x, cond, pos0, prefix_k, prefix_v, h_re, h_im, n_keep,
                w_ada, b_ada, norm_g, w_in, a_re, a_im, log_dt, b_re, b_im,
                cm_re, cm_im, d_skip, w_glu, b_glu, sinks, w_out):
    bsz, L = x.shape[0], x.shape[1]
    mod = jax.nn.silu(cond) @ w_ada + b_ada
    shift, scale, gate = jnp.split(mod, 3, axis=-1)
    h = rmsnorm(x, norm_g) * (1.0 + scale[:, None]) + shift[:, None]
    z = h @ w_in
    u, g_ssm, q, k, v, g_att = jnp.split(z, SPLITS, axis=-1)

    lr, li, bbr, bbi = ssm_discretise(a_re, a_im, log_dt, b_re, b_im)
    uf = u.astype(jnp.float32).reshape(bsz, L, N_SSM_GROUPS, SSM_GROUP)
    y, hr, hi = ssm_scan(uf, h_re.astype(jnp.float32), h_im.astype(jnp.float32), lr, li, bbr, bbi,
                         cm_re.astype(jnp.float32), cm_im.astype(jnp.float32), d_skip.astype(jnp.float32))
    y = jax.nn.gelu(y.reshape(bsz, L, D_SSM).astype(x.dtype))
    y_ssm = y * jax.nn.sigmoid(y @ w_glu + b_glu) * jax.nn.silu(g_ssm)

    pos = pos0 + jnp.arange(L)
    q = rope(q.reshape(bsz, L, N_HEADS, HEAD_DIM), pos)
    k = rope(k.reshape(bsz, L, N_KV_HEADS, HEAD_DIM), pos)
    v = v.reshape(bsz, L, N_KV_HEADS, HEAD_DIM)
    pad = WINDOW - prefix_k.shape[1]
    if pad > 0:
        prefix_k = jnp.pad(prefix_k, ((0, 0), (pad, 0), (0, 0), (0, 0)))
        prefix_v = jnp.pad(prefix_v, ((0, 0), (pad, 0), (0, 0), (0, 0)))
    k_all = jnp.concatenate([prefix_k.astype(k.dtype), k], axis=1)
    v_all = jnp.concatenate([prefix_v.astype(v.dtype), v], axis=1)
    o = window_attention(q, k_all, v_all, pos0, sinks) * jax.nn.silu(g_att)

    out = jnp.concatenate([y_ssm, o.astype(y_ssm.dtype)], axis=-1) @ w_out
    x = x + gate[:, None] * out
    return x, k_all[:, -n_keep:], v_all[:, -n_keep:], hr, hi


def run_group(x, cond, pos0, prefix_k, prefix_v, h_re, h_im, n_keep,
              w_ada, b_ada, norm_g, w_in, ssm_a_re, ssm_a_im, ssm_log_dt, ssm_b_re, ssm_b_im,
              ssm_c_re, ssm_c_im, ssm_d, w_glu, b_glu, attn_sinks, w_out, final_g):
    ks, vs, hrs, his = [], [], [], []
    for i in range(DEPTH):
        x, nk, nv, hr, hi = mixer_layer(
            x, cond, pos0, prefix_k[i], prefix_v[i], h_re[i], h_im[i], n_keep,
            w_ada[i], b_ada[i], norm_g[i], w_in[i], ssm_a_re[i], ssm_a_im[i], ssm_log_dt[i],
            ssm_b_re[i], ssm_b_im[i], ssm_c_re[i], ssm_c_im[i], ssm_d[i], w_glu[i], b_glu[i],
            attn_sinks[i], w_out[i])
        ks.append(nk)
        vs.append(nv)
        hrs.append(hr)
        his.append(hi)
    return rmsnorm(x, final_g), jnp.stack(ks), jnp.stack(vs), jnp.stack(hrs), jnp.stack(his)


def setup_inputs(seed: int = 0) -> dict:
    key = jax.random.key(seed)
    ks = jax.random.split(key, 26)
    f32 = jnp.float32

    def nrm(k, shape, s):
        return jax.random.normal(k, shape, f32) * s

    G, P, C = N_SSM_GROUPS, SSM_STATE, SSM_GROUP
    win_buf = min(WINDOW, PAST_LEN)
    n_idx = jnp.arange(P, dtype=f32)
    return {
        'x_prompt': nrm(ks[0], (BATCH, SEQ, D_MODEL), 1.0),
        'x_sample': nrm(ks[1], (DEC_BATCH, DEC_SEQ, D_MODEL), 1.0),
        'c_prompt': nrm(ks[2], (BATCH, D_MODEL), 1.0),
        'c_sample': nrm(ks[3], (DEC_BATCH, D_MODEL), 1.0),
        'cache_win_k': nrm(ks[4], (DEPTH, DEC_BATCH, win_buf, N_KV_HEADS, HEAD_DIM), 1.0),
        'cache_win_v': nrm(ks[5], (DEPTH, DEC_BATCH, win_buf, N_KV_HEADS, HEAD_DIM), 1.0),
        'state_ssm_re': nrm(ks[6], (DEPTH, DEC_BATCH, G, P), 0.1),
        'state_ssm_im': nrm(ks[7], (DEPTH, DEC_BATCH, G, P), 0.1),
        'w_ada': nrm(ks[8], (DEPTH, D_MODEL, 3 * D_MODEL), 0.5 * D_MODEL ** -0.5),
        'b_ada': nrm(ks[9], (DEPTH, 3 * D_MODEL), 0.02),
        'norm_g': 1.0 + nrm(ks[10], (DEPTH, D_MODEL), 0.02),
        'w_in': nrm(ks[11], (DEPTH, D_MODEL, D_IN), D_MODEL ** -0.5),
        'ssm_a_re': -0.5 + nrm(ks[12], (DEPTH, G, P), 0.01),
        'ssm_a_im': math.pi * n_idx + nrm(ks[13], (DEPTH, G, P), 0.01),
        'ssm_log_dt': jax.random.uniform(ks[14], (DEPTH, G), f32, math.log(DT_MIN), math.log(DT_MAX)),
        'ssm_b_re': nrm(ks[15], (DEPTH, G, P, C), (2.0 * C) ** -0.5),
        'ssm_b_im': nrm(ks[16], (DEPTH, G, P, C), (2.0 * C) ** -0.5),
        'ssm_c_re': nrm(ks[17], (DEPTH, G, C, P), (2.0 * P) ** -0.5),
        'ssm_c_im': nrm(ks[18], (DEPTH, G, C, P), (2.0 * P) ** -0.5),
        'ssm_d': nrm(ks[19], (DEPTH, D_SSM), 1.0),
        'w_glu': nrm(ks[20], (DEPTH, D_SSM, D_SSM), D_SSM ** -0.5),
        'b_glu': nrm(ks[21], (DEPTH, D_SSM), 0.02),
        'attn_sinks': nrm(ks[22], (DEPTH, N_HEADS), 0.5),
        'w_out': nrm(ks[23], (DEPTH, D_MIX, D_MODEL), D_MIX ** -0.5),
        'final_g': 1.0 + nrm(ks[24], (D_MODEL,), 0.02),
    }


def reference(x_prompt, x_sample, c_prompt, c_sample, cache_win_k, cache_win_v,
              state_ssm_re, state_ssm_im, w_ada, b_ada, norm_g, w_in, ssm_a_re, ssm_a_im,
              ssm_log_dt, ssm_b_re, ssm_b_im, ssm_c_re, ssm_c_im, ssm_d, w_glu, b_glu,
              attn_sinks, w_out, final_g):
    bp = x_prompt.shape[0]
    zero_k = jnp.zeros((DEPTH, bp, WINDOW, N_KV_HEADS, HEAD_DIM), x_prompt.dtype)
    zero_h = jnp.zeros((DEPTH, bp, N_SSM_GROUPS, SSM_STATE), jnp.float32)
    y_prompt, pk, pv, pr, pi = run_group(
        x_prompt, c_prompt, 0, zero_k, zero_k, zero_h, zero_h, min(WINDOW, x_prompt.shape[1]),
        w_ada, b_ada, norm_g, w_in, ssm_a_re, ssm_a_im, ssm_log_dt, ssm_b_re, ssm_b_im,
        ssm_c_re, ssm_c_im, ssm_d, w_glu, b_glu, attn_sinks, w_out, final_g)
    y_sample, sk, sv, sr, si = run_group(
        x_sample, c_sample, PAST_LEN, cache_win_k, cache_win_v, state_ssm_re, state_ssm_im,
        cache_win_k.shape[2],
        w_ada, b_ada, norm_g, w_in, ssm_a_re, ssm_a_im, ssm_log_dt, ssm_b_re, ssm_b_im,
        ssm_c_re, ssm_c_im, ssm_d, w_glu, b_glu, attn_sinks, w_out, final_g)
    return (y_prompt, y_sample, pk, pv, pr, pi, sk, sv, sr, si)
```

```python
import functools
import math

import jax
import jax.numpy as jnp
from jax import lax
from jax.experimental import pallas as pl
from jax.experimental.pallas import tpu as pltpu

F32 = jnp.float32
BF16 = jnp.bfloat16

D_MODEL = 4096
DEPTH = 4
D_SSM = 2048
D_ATT = 2048
SSM_GROUP = 16
N_GROUPS = 128
SSM_STATE = 64
HEAD_DIM = 128
N_HEADS = 16
N_KV_HEADS = 4
KV_GROUP = 4
D_KV = 512
WINDOW = 128
ROPE_THETA = 10000.0
EPS = 1e-6
BLOCK = 128
D_IN = 9216
ATT_SCALE = HEAD_DIM ** -0.5

IN_TN = 512
IN_TILES = D_IN // IN_TN
U_TILES = D_SSM // IN_TN
ZB_COLS = D_IN - D_SSM
ZB_Q, ZB_GA, ZB_GS, ZB_K, ZB_V = 0, 2048, 4096, 6144, 6656

CHUNK = 16
SSM_GB = 8
SSM_W = SSM_GB * SSM_GROUP
MOD_ROWS = 40

VMEM_LIMIT_BYTES = 56 * 1024 * 1024


def _params(*sem):
    return pltpu.CompilerParams(dimension_semantics=sem, vmem_limit_bytes=VMEM_LIMIT_BYTES)


def _mod_body(c_ref, w_ref, b_ref, o_ref):
    a = jax.nn.silu(c_ref[...]).astype(BF16)
    o_ref[...] = jnp.dot(a, w_ref[...].astype(BF16), preferred_element_type=F32) + b_ref[...]


def _modulation(c_all, w_ada, b_ada):
    tn = 512
    return pl.pallas_call(
        _mod_body,
        grid=(DEPTH, 3 * D_MODEL // tn),
        in_specs=[pl.BlockSpec((MOD_ROWS, D_MODEL), lambda l, j: (0, 0)),
                  pl.BlockSpec((None, D_MODEL, tn), lambda l, j: (l, 0, j)),
                  pl.BlockSpec((None, 1, tn), lambda l, j: (l, 0, j))],
        out_specs=pl.BlockSpec((None, MOD_ROWS, tn), lambda l, j: (l, 0, j)),
        out_shape=jax.ShapeDtypeStruct((DEPTH, MOD_ROWS, 3 * D_MODEL), F32),
        compiler_params=_params("arbitrary", "arbitrary"),
        name="adaln_mod",
    )(c_all, w_ada, b_ada.reshape(DEPTH, 1, 3 * D_MODEL))


def _rmsmod_body(x_ref, g_ref, sc_ref, sh_ref, o_ref):
    x = x_ref[...]
    y = x * lax.rsqrt(jnp.mean(x * x, axis=-1, keepdims=True) + EPS)
    o_ref[...] = ((y * g_ref[...]) * (1.0 + sc_ref[...]) + sh_ref[...]).astype(BF16)


def _rms_mod(x3, norm_g3, mod4, l, tm):
    bt, lr, _ = x3.shape
    r = mod4.shape[2]
    return pl.pallas_call(
        _rmsmod_body,
        grid=(bt, lr // tm),
        in_specs=[pl.BlockSpec((None, tm, D_MODEL), lambda b, i: (b, i, 0)),
                  pl.BlockSpec((None, 1, D_MODEL), lambda b, i: (l, 0, 0)),
                  pl.BlockSpec((None, None, r, D_MODEL), lambda b, i: (l, b, 0, 1)),
                  pl.BlockSpec((None, None, r, D_MODEL), lambda b, i: (l, b, 0, 0))],
        out_specs=pl.BlockSpec((None, tm, D_MODEL), lambda b, i: (b, i, 0)),
        out_shape=jax.ShapeDtypeStruct((bt, lr, D_MODEL), BF16),
        compiler_params=_params("arbitrary", "arbitrary"),
        name="rms_mod",
    )(x3, norm_g3, mod4, mod4)


def _rope(x, cos, sin):
    return x * cos + pltpu.roll(x, HEAD_DIM // 2, 1) * sin


def _inproj_body(h_ref, w_ref, cos_ref, sin_ref, u_ref, zb_ref):
    j = pl.program_id(1)
    acc = jnp.dot(h_ref[...], w_ref[...], preferred_element_type=F32)

    @pl.when(j < 4)
    def _():
        u_ref[...] = acc

    @pl.when(((j >= 4) & (j < 8)) | (j == 16))
    def _():
        cos = cos_ref[...]
        sin = sin_ref[...]
        for hh in range(IN_TN // HEAD_DIM):
            sl = slice(hh * HEAD_DIM, (hh + 1) * HEAD_DIM)
            zb_ref[:, sl] = _rope(acc[:, sl], cos, sin).astype(BF16)

    @pl.when((j >= 8) & (j < 16))
    def _():
        zb_ref[...] = jax.nn.silu(acc).astype(BF16)

    @pl.when(j == 17)
    def _():
        zb_ref[...] = acc.astype(BF16)


def _in_proj(h2, w_in_b, cos_t, sin_t, l, tm):
    m = h2.shape[0]
    nt = cos_t.shape[0] // tm
    return pl.pallas_call(
        _inproj_body,
        grid=(m // tm, IN_TILES),
        in_specs=[pl.BlockSpec((tm, D_MODEL), lambda i, j: (i, 0)),
                  pl.BlockSpec((None, D_MODEL, IN_TN), lambda i, j: (l, 0, j)),
                  pl.BlockSpec((tm, HEAD_DIM), lambda i, j: (i % nt, 0)),
                  pl.BlockSpec((tm, HEAD_DIM), lambda i, j: (i % nt, 0))],
        out_specs=[pl.BlockSpec((tm, IN_TN), lambda i, j: (i, jnp.minimum(j, U_TILES - 1))),
                   pl.BlockSpec((tm, IN_TN), lambda i, j: (i, jnp.maximum(j - U_TILES, 0)))],
        out_shape=[jax.ShapeDtypeStruct((m, D_SSM), F32),
                   jax.ShapeDtypeStruct((m, ZB_COLS), BF16)],
        compiler_params=_params("arbitrary", "arbitrary"),
        name="in_proj",
    )(h2, w_in_b, cos_t, sin_t)


def _cmul(ar, ai, br, bi):
    return ar * br - ai * bi, ar * bi + ai * br


def _discretise(ar, ai, log_dt):
    dt = jnp.exp(log_dt)
    mag = jnp.exp(ar * dt)
    lr = mag * jnp.cos(ai * dt)
    li = mag * jnp.sin(ai * dt)
    nr = lr - 1.0
    ni = li
    den = ar * ar + ai * ai
    qr = (nr * ar + ni * ai) / den
    qi = (ni * ar - nr * ai) / den
    return lr, li, qr, qi


def _powers(lr, li, n):
    pr, pi = [jnp.ones_like(lr)], [jnp.zeros_like(li)]
    for _ in range(n):
        r, i = _cmul(pr[-1], pi[-1], lr, li)
        pr.append(r)
        pi.append(i)
    return pr, pi


def _ssm_prep_body(arr_ref, air_ref, arc_ref, aic_ref, ldt_ref, btr_ref, bti_ref, ctr_ref, cti_ref,
                   mt_ref, wsr_ref, wsi_ref, wcr_ref, wci_ref, l16r_ref, l16i_ref):
    gb, t_len, p = SSM_GB, CHUNK, SSM_STATE
    kdim = t_len * SSM_GROUP
    ldt = ldt_ref[...]
    lr_r, li_r, _, _ = _discretise(arr_ref[...], air_ref[...], ldt)
    lr_c, li_c, qr_c, qi_c = _discretise(arc_ref[...], aic_ref[...], ldt)

    pr, pi = _powers(lr_r, li_r, t_len)
    bshape = (gb, SSM_GROUP, p)
    pow0_r = jnp.concatenate([jnp.broadcast_to(pr[t], bshape) for t in range(t_len)], axis=1)
    pow0_i = jnp.concatenate([jnp.broadcast_to(pi[t], bshape) for t in range(t_len)], axis=1)
    pow1_r = jnp.concatenate([jnp.broadcast_to(pr[t + 1], bshape) for t in range(t_len)], axis=1)
    pow1_i = jnp.concatenate([jnp.broadcast_to(pi[t + 1], bshape) for t in range(t_len)], axis=1)

    cr, ci = _powers(lr_c, li_c, t_len)
    mag2 = lr_c * lr_c + li_c * li_c
    ir, ii = _powers(lr_c / mag2, -li_c / mag2, t_len - 1)
    sidx = lax.broadcasted_iota(jnp.int32, (gb, p, kdim), 2) >> 4
    zero = jnp.zeros((gb, p, kdim), F32)
    linv_r, linv_i, l15_r, l15_i = zero, zero, zero, zero
    for s in range(t_len):
        sel = sidx == s
        linv_r = jnp.where(sel, ir[s], linv_r)
        linv_i = jnp.where(sel, ii[s], linv_i)
        l15_r = jnp.where(sel, cr[t_len - 1 - s], l15_r)
        l15_i = jnp.where(sel, ci[t_len - 1 - s], l15_i)

    bbr, bbi = _cmul(qr_c, qi_c, btr_ref[...], bti_ref[...])
    blr, bli = _cmul(linv_r, linv_i, bbr, bbi)
    wsr, wsi = _cmul(l15_r, l15_i, bbr, bbi)
    ctr, cti = ctr_ref[...], cti_ref[...]
    clr, cli = _cmul(ctr, cti, pow0_r, pow0_i)
    wcr, wci = _cmul(ctr, cti, pow1_r, pow1_i)

    row_t = lax.broadcasted_iota(jnp.int32, (kdim, kdim), 0) >> 4
    col_s = lax.broadcasted_iota(jnp.int32, (kdim, kdim), 1) >> 4
    causal = row_t >= col_s
    hi = lax.Precision.HIGHEST
    for g in range(gb):
        m = (jnp.dot(clr[g], blr[g], precision=hi, preferred_element_type=F32)
             - jnp.dot(cli[g], bli[g], precision=hi, preferred_element_type=F32))
        mt_ref[g] = jnp.where(causal, m, 0.0).astype(BF16)
    wsr_ref[...] = wsr.astype(BF16)
    wsi_ref[...] = wsi.astype(BF16)
    wcr_ref[...] = wcr.astype(BF16)
    wci_ref[...] = (-wci).astype(BF16)
    l16r_ref[...] = cr[t_len]
    l16i_ref[...] = ci[t_len]


def _ssm_prep(a_re, a_im, log_dt, b_re, b_im, c_re, c_im):
    g, p, kdim, gb = N_GROUPS, SSM_STATE, CHUNK * SSM_GROUP, SSM_GB
    arr = a_re.reshape(DEPTH, g, 1, p)
    air = a_im.reshape(DEPTH, g, 1, p)
    arc = a_re.reshape(DEPTH, g, p, 1)
    aic = a_im.reshape(DEPTH, g, p, 1)
    ldt = log_dt.reshape(DEPTH, g, 1, 1)
    btr = jnp.tile(b_re, (1, 1, 1, CHUNK))
    bti = jnp.tile(b_im, (1, 1, 1, CHUNK))
    ctr = jnp.tile(c_re, (1, 1, CHUNK, 1))
    cti = jnp.tile(c_im, (1, 1, CHUNK, 1))

    def spec(a, b):
        return pl.BlockSpec((None, gb, a, b), lambda l, i: (l, i, 0, 0))

    return pl.pallas_call(
        _ssm_prep_body,
        grid=(DEPTH, g // gb),
        in_specs=[spec(1, p), spec(1, p), spec(p, 1), spec(p, 1), spec(1, 1),
                  spec(p, kdim), spec(p, kdim), spec(kdim, p), spec(kdim, p)],
        out_specs=[spec(kdim, kdim), spec(p, kdim), spec(p, kdim), spec(kdim, p), spec(kdim, p),
                   spec(p, 1), spec(p, 1)],
        out_shape=[jax.ShapeDtypeStruct((DEPTH, g, kdim, kdim), BF16),
                   jax.ShapeDtypeStruct((DEPTH, g, p, kdim), BF16),
                   jax.ShapeDtypeStruct((DEPTH, g, p, kdim), BF16),
                   jax.ShapeDtypeStruct((DEPTH, g, kdim, p), BF16),
                   jax.ShapeDtypeStruct((DEPTH, g, kdim, p), BF16),
                   jax.ShapeDtypeStruct((DEPTH, g, p, 1), F32),
                   jax.ShapeDtypeStruct((DEPTH, g, p, 1), F32)],
        compiler_params=_params("arbitrary", "arbitrary"),
        name="ssm_prep",
    )(arr, air, arc, aic, ldt, btr, bti, ctr, cti)


def _ssm_body(u_ref, mt_ref, wsr_ref, wsi_ref, wcr_ref, wci_ref, l16r_ref, l16i_ref, d_ref,
              yg_ref, hr_ref, hi_ref, ybuf):
    gb, t_len, p = SSM_GB, CHUNK, SSM_STATE
    r_len = u_ref.shape[0] // t_len
    slab_t = [u_ref[pl.ds(s, r_len, stride=t_len), :].T.astype(BF16) for s in range(t_len)]

    yl, xr, xi = [], [], []
    for g in range(gb):
        rows = slice(g * SSM_GROUP, (g + 1) * SSM_GROUP)
        s_g = jnp.concatenate([st[rows, :] for st in slab_t], axis=0)
        yl.append(jnp.dot(mt_ref[g], s_g, preferred_element_type=F32))
        xr.append(jnp.dot(wsr_ref[g], s_g, preferred_element_type=F32))
        xi.append(jnp.dot(wsi_ref[g], s_g, preferred_element_type=F32))
    h_r = jnp.concatenate(xr, axis=0)
    h_i = jnp.concatenate(xi, axis=0)

    a_r = l16r_ref[...].reshape(gb * p, 1)
    a_i = l16i_ref[...].reshape(gb * p, 1)
    lane = lax.broadcasted_iota(jnp.int32, (gb * p, r_len), 1)
    k = 1
    while k < r_len:
        keep = lane >= k
        s_r = jnp.where(keep, pltpu.roll(h_r, k, 1), 0.0)
        s_i = jnp.where(keep, pltpu.roll(h_i, k, 1), 0.0)
        h_r, h_i = h_r + a_r * s_r - a_i * s_i, h_i + a_r * s_i + a_i * s_r
        a_r, a_i = a_r * a_r - a_i * a_i, 2.0 * a_r * a_i
        k *= 2
    hr_ref[...] = h_r[:, r_len - 1:r_len].reshape(gb, p, 1)
    hi_ref[...] = h_i[:, r_len - 1:r_len].reshape(gb, p, 1)

    first = lane >= 1
    c_r = jnp.where(first, pltpu.roll(h_r, 1, 1), 0.0).astype(BF16)
    c_i = jnp.where(first, pltpu.roll(h_i, 1, 1), 0.0).astype(BF16)
    ys = []
    for g in range(gb):
        rows = slice(g * p, (g + 1) * p)
        yc = (jnp.dot(wcr_ref[g], c_r[rows, :], preferred_element_type=F32)
              + jnp.dot(wci_ref[g], c_i[rows, :], preferred_element_type=F32))
        ys.append(yl[g] + yc)
    d = d_ref[...]
    for t in range(t_len):
        rows = slice(t * SSM_GROUP, (t + 1) * SSM_GROUP)
        y_t = jnp.concatenate([y[rows, :] for y in ys], axis=0)
        u_t = u_ref[pl.ds(t, r_len, stride=t_len), :]
        ybuf[pl.ds(t, r_len, stride=t_len), :] = jax.nn.gelu(y_t.T + d * u_t)
    yg_ref[...] = ybuf[...].astype(BF16)


def _ssm_prompt(u3, prep, ssm_d3, l):
    b, seq, _ = u3.shape
    mt, wsr, wsi, wcr, wci, l16r, l16i = prep
    gb, p, kdim, w = SSM_GB, SSM_STATE, CHUNK * SSM_GROUP, SSM_W

    def wspec(a, c):
        return pl.BlockSpec((None, gb, a, c), lambda bb, i: (l, i, 0, 0))

    st_spec = pl.BlockSpec((None, gb, p, 1), lambda bb, i: (bb, i, 0, 0))
    return pl.pallas_call(
        _ssm_body,
        grid=(b, N_GROUPS // gb),
        in_specs=[pl.BlockSpec((None, seq, w), lambda bb, i: (bb, 0, i)),
                  wspec(kdim, kdim), wspec(p, kdim), wspec(p, kdim), wspec(kdim, p), wspec(kdim, p),
                  wspec(p, 1), wspec(p, 1),
                  pl.BlockSpec((None, 1, w), lambda bb, i: (l, 0, i))],
        out_specs=[pl.BlockSpec((None, seq, w), lambda bb, i: (bb, 0, i)), st_spec, st_spec],
        out_shape=[jax.ShapeDtypeStruct((b, seq, D_SSM), BF16),
                   jax.ShapeDtypeStruct((b, N_GROUPS, p, 1), F32),
                   jax.ShapeDtypeStruct((b, N_GROUPS, p, 1), F32)],
        scratch_shapes=[pltpu.VMEM((seq, w), F32)],
        compiler_params=_params("arbitrary", "arbitrary"),
        name="ssm_scan",
    )(u3, mt, wsr, wsi, wcr, wci, l16r, l16i, ssm_d3)


SSM_STEP_GB = 16


def _ssm_step_body(arc_ref, aic_ref, ldt_ref, br_ref, bi_ref, cr_ref, ci_ref, d_ref, ut_ref,
                   hr_ref, hi_ref, yt_ref, nhr_ref, nhi_ref):
    lr, li, qr, qi = _discretise(arc_ref[...], aic_ref[...], ldt_ref[...])
    bbr, bbi = _cmul(qr, qi, br_ref[...], bi_ref[...])
    for g in range(SSM_STEP_GB):
        u = ut_ref[g]
        ub = u.astype(BF16)
        x_r = jnp.dot(bbr[g].astype(BF16), ub, preferred_element_type=F32)
        x_i = jnp.dot(bbi[g].astype(BF16), ub, preferred_element_type=F32)
        h_r, h_i = hr_ref[g], hi_ref[g]
        n_r = lr[g] * h_r - li[g] * h_i + x_r
        n_i = lr[g] * h_i + li[g] * h_r + x_i
        y = (jnp.dot(cr_ref[g].astype(BF16), n_r.astype(BF16), preferred_element_type=F32)
             - jnp.dot(ci_ref[g].astype(BF16), n_i.astype(BF16), preferred_element_type=F32))
        yt_ref[g] = jax.nn.gelu(y + d_ref[g] * u)
        nhr_ref[g] = n_r
        nhi_ref[g] = n_i


def _ssm_step(ut, ht_r, ht_i, a_re, a_im, log_dt, b_re, b_im, c_re, c_im, ssm_d, l):
    g, p, gb = N_GROUPS, SSM_STATE, SSM_STEP_GB
    bs = ut.shape[-1]

    def lspec(a, c):
        return pl.BlockSpec((None, gb, a, c), lambda i: (l, i, 0, 0))

    def aspec(a):
        return pl.BlockSpec((gb, a, bs), lambda i: (i, 0, 0))

    return pl.pallas_call(
        _ssm_step_body,
        grid=(g // gb,),
        in_specs=[lspec(p, 1), lspec(p, 1), lspec(1, 1), lspec(p, SSM_GROUP), lspec(p, SSM_GROUP),
                  lspec(SSM_GROUP, p), lspec(SSM_GROUP, p), lspec(SSM_GROUP, 1),
                  aspec(SSM_GROUP), aspec(p), aspec(p)],
        out_specs=[aspec(SSM_GROUP), aspec(p), aspec(p)],
        out_shape=[jax.ShapeDtypeStruct((g, SSM_GROUP, bs), F32),
                   jax.ShapeDtypeStruct((g, p, bs), F32),
                   jax.ShapeDtypeStruct((g, p, bs), F32)],
        compiler_params=_params("arbitrary"),
        name="ssm_step",
    )(a_re.reshape(DEPTH, g, p, 1), a_im.reshape(DEPTH, g, p, 1), log_dt.reshape(DEPTH, g, 1, 1),
      b_re, b_im, c_re, c_im, ssm_d.reshape(DEPTH, g, SSM_GROUP, 1), ut, ht_r, ht_i)


def _attn_body(l, q_ref, ga_ref, kc_ref, kp_ref, vc_ref, vp_ref, sink_ref, o_ref):
    n = pl.program_id(1)
    rows = KV_GROUP * BLOCK
    r = lax.broadcasted_iota(jnp.int32, (rows, 2 * BLOCK), 0) & (BLOCK - 1)
    cc = lax.broadcasted_iota(jnp.int32, (rows, 2 * BLOCK), 1)
    no_prev = jnp.where(n > 0, 0, 2 * BLOCK)
    valid = ((cc < BLOCK) & (cc > r + no_prev)) | ((cc >= BLOCK) & (cc - BLOCK <= r))
    for kh in range(N_KV_HEADS):
        ksl = slice(kh * HEAD_DIM, (kh + 1) * HEAD_DIM)
        k = jnp.concatenate([kp_ref[:, ksl], kc_ref[:, ksl]], axis=0)
        v = jnp.concatenate([vp_ref[:, ksl], vc_ref[:, ksl]], axis=0)
        heads = [kh * KV_GROUP + g for g in range(KV_GROUP)]
        q4 = jnp.concatenate([q_ref[:, h * HEAD_DIM:(h + 1) * HEAD_DIM] for h in heads], axis=0)
        s = lax.dot_general(q4, k, (((1,), (1,)), ((), ())), preferred_element_type=F32) * ATT_SCALE
        s = jnp.where(valid, s, -jnp.inf)
        sink = jnp.concatenate([jnp.full((BLOCK, 1), sink_ref[l, h], F32) for h in heads], axis=0)
        m = jnp.maximum(jnp.max(s, axis=-1, keepdims=True), sink)
        e = jnp.exp(s - m)
        den = jnp.sum(e, axis=-1, keepdims=True) + jnp.exp(sink - m)
        o = jnp.dot((e / den).astype(BF16), v, preferred_element_type=F32)
        for g, h in enumerate(heads):
            hsl = slice(h * HEAD_DIM, (h + 1) * HEAD_DIM)
            o_ref[:, hsl] = (o[g * BLOCK:(g + 1) * BLOCK, :] * ga_ref[:, hsl].astype(F32)).astype(BF16)


def _attention_prompt(zb3, sinks, l):
    b, seq, _ = zb3.shape
    nb = seq // BLOCK
    wide = pl.BlockSpec((None, BLOCK, D_ATT), lambda bb, n: (bb, n, ZB_Q // D_ATT))
    gate = pl.BlockSpec((None, BLOCK, D_ATT), lambda bb, n: (bb, n, ZB_GA // D_ATT))

    def kv(col, prev):
        if prev:
            return pl.BlockSpec((None, BLOCK, D_KV), lambda bb, n: (bb, jnp.maximum(n - 1, 0), col // D_KV))
        return pl.BlockSpec((None, BLOCK, D_KV), lambda bb, n: (bb, n, col // D_KV))

    return pl.pallas_call(
        functools.partial(_attn_body, l),
        grid=(b, nb),
        in_specs=[wide, gate, kv(ZB_K, False), kv(ZB_K, True), kv(ZB_V, False), kv(ZB_V, True),
                  pl.BlockSpec(memory_space=pltpu.SMEM)],
        out_specs=pl.BlockSpec((None, BLOCK, D_ATT), lambda bb, n: (bb, n, 0)),
        out_shape=jax.ShapeDtypeStruct((b, seq, D_ATT), BF16),
        compiler_params=_params("arbitrary", "arbitrary"),
        name="swa_prompt",
    )(zb3, zb3, zb3, zb3, zb3, zb3, sinks)


def _attn_step_body(l, q_ref, ga_ref, kn_ref, vn_ref, ck_ref, cv_ref, sink_ref, o_ref):
    col = lax.broadcasted_iota(jnp.int32, (KV_GROUP, WINDOW), 1)
    for kh in range(N_KV_HEADS):
        ksl = slice(kh * HEAD_DIM, (kh + 1) * HEAD_DIM)
        kc = ck_ref[:, ksl].astype(BF16)
        vc = cv_ref[:, ksl].astype(BF16)
        q4 = q_ref[kh].astype(BF16)
        kn = kn_ref[kh].astype(BF16).astype(F32)
        vn = vn_ref[kh].astype(BF16).astype(F32)
        s = lax.dot_general(q4, kc, (((1,), (1,)), ((), ())), preferred_element_type=F32) * ATT_SCALE
        s = jnp.where(col >= 1, s, -jnp.inf)
        s_new = jnp.sum(q4.astype(F32) * kn, axis=-1, keepdims=True) * ATT_SCALE
        sink = jnp.concatenate([jnp.full((1, 1), sink_ref[l, kh * KV_GROUP + g], F32)
                                for g in range(KV_GROUP)], axis=0)
        m = jnp.maximum(jnp.maximum(jnp.max(s, axis=-1, keepdims=True), s_new), sink)
        e = jnp.exp(s - m)
        e_new = jnp.exp(s_new - m)
        den = jnp.sum(e, axis=-1, keepdims=True) + e_new + jnp.exp(sink - m)
        o = (jnp.dot((e / den).astype(BF16), vc, preferred_element_type=F32)
             + (e_new / den).astype(BF16).astype(F32) * vn)
        o_ref[kh] = o * ga_ref[kh]


def _attention_step(q4, ga4, kn4, vn4, cache_k2, cache_v2, sinks, l):
    bs = q4.shape[0]
    hspec = pl.BlockSpec((None, N_KV_HEADS, KV_GROUP, HEAD_DIM), lambda b: (b, 0, 0, 0))
    nspec = pl.BlockSpec((None, N_KV_HEADS, 1, HEAD_DIM), lambda b: (b, 0, 0, 0))
    cspec = pl.BlockSpec((None, None, WINDOW, D_KV), lambda b: (l, b, 0, 0))
    return pl.pallas_call(
        functools.partial(_attn_step_body, l),
        grid=(bs,),
        in_specs=[hspec, hspec, nspec, nspec, cspec, cspec, pl.BlockSpec(memory_space=pltpu.SMEM)],
        out_specs=hspec,
        out_shape=jax.ShapeDtypeStruct((bs, N_KV_HEADS, KV_GROUP, HEAD_DIM), F32),
        compiler_params=_params("arbitrary"),
        name="swa_step",
    )(q4, ga4, kn4, vn4, cache_k2, cache_v2, sinks)


def _glu_body(y_ref, w_ref, b_ref, gs_ref, o_ref):
    y = y_ref[...]
    acc = jnp.dot(y, w_ref[...], preferred_element_type=F32) + b_ref[...]
    o_ref[...] = (y.astype(F32) * jax.nn.sigmoid(acc) * gs_ref[...].astype(F32)).astype(BF16)


def _glu(yg2, zb2, w_glu_b, b_glu3, l, tm):
    m = yg2.shape[0]
    return pl.pallas_call(
        _glu_body,
        grid=(m // tm,),
        in_specs=[pl.BlockSpec((tm, D_SSM), lambda i: (i, 0)),
                  pl.BlockSpec((None, D_SSM, D_SSM), lambda i: (l, 0, 0)),
                  pl.BlockSpec((None, 1, D_SSM), lambda i: (l, 0, 0)),
                  pl.BlockSpec((tm, D_SSM), lambda i: (i, ZB_GS // D_SSM))],
        out_specs=pl.BlockSpec((tm, D_SSM), lambda i: (i, 0)),
        out_shape=jax.ShapeDtypeStruct((m, D_SSM), BF16),
        compiler_params=_params("arbitrary"),
        name="glu",
    )(yg2, w_glu_b, b_glu3, zb2)


def _outproj_body(a_ref, b_ref, w1_ref, w2_ref, x_ref, g_ref, o_ref):
    acc = (jnp.dot(a_ref[...], w1_ref[...], preferred_element_type=F32)
           + jnp.dot(b_ref[...], w2_ref[...], preferred_element_type=F32))
    o_ref[...] = x_ref[...] + g_ref[...] * acc


def _out_proj(ys2, oa2, w_out_b4, x2, mod4, l, tm, rows_per_batch):
    m = x2.shape[0]
    tn = 1024
    r = mod4.shape[2]
    per_b = rows_per_batch // tm
    gate_blk = 2 * D_MODEL // tn
    return pl.pallas_call(
        _outproj_body,
        grid=(D_MODEL // tn, m // tm),
        in_specs=[pl.BlockSpec((tm, D_SSM), lambda j, i: (i, 0)),
                  pl.BlockSpec((tm, D_ATT), lambda j, i: (i, 0)),
                  pl.BlockSpec((None, None, D_SSM, tn), lambda j, i: (l, 0, 0, j)),
                  pl.BlockSpec((None, None, D_ATT, tn), lambda j, i: (l, 1, 0, j)),
                  pl.BlockSpec((tm, tn), lambda j, i: (i, j)),
                  pl.BlockSpec((None, None, r, tn), lambda j, i: (l, i // per_b, 0, gate_blk + j))],
        out_specs=pl.BlockSpec((tm, tn), lambda j, i: (i, j)),
        out_shape=jax.ShapeDtypeStruct((m, D_MODEL), F32),
        compiler_params=_params("arbitrary", "arbitrary"),
        name="out_proj",
    )(ys2, oa2, w_out_b4, w_out_b4, x2, mod4)


def _final_norm_body(x_ref, g_ref, o_ref):
    x = x_ref[...]
    y = x * lax.rsqrt(jnp.mean(x * x, axis=-1, keepdims=True) + EPS)
    o_ref[...] = y * g_ref[...]


def _final_norm(x2, final_g2, tm):
    m = x2.shape[0]
    return pl.pallas_call(
        _final_norm_body,
        grid=(m // tm,),
        in_specs=[pl.BlockSpec((tm, D_MODEL), lambda i: (i, 0)),
                  pl.BlockSpec((1, D_MODEL), lambda i: (0, 0))],
        out_specs=pl.BlockSpec((tm, D_MODEL), lambda i: (i, 0)),
        out_shape=jax.ShapeDtypeStruct((m, D_MODEL), F32),
        compiler_params=_params("arbitrary"),
        name="final_norm",
    )(x2, final_g2)


def _rope_tables(pos):
    half = HEAD_DIM // 2
    inv = ROPE_THETA ** (-jnp.arange(half, dtype=F32) / half)
    ang = pos.astype(F32)[:, None] * inv[None, :]
    cos, sin = jnp.cos(ang), jnp.sin(ang)
    return jnp.concatenate([cos, cos], axis=-1), jnp.concatenate([-sin, sin], axis=-1)


def kernel(x_prompt, x_sample, c_prompt, c_sample, cache_win_k, cache_win_v, state_ssm_re, state_ssm_im, w_ada, b_ada, norm_g, w_in, ssm_a_re, ssm_a_im, ssm_log_dt, ssm_b_re, ssm_b_im, ssm_c_re, ssm_c_im, ssm_d, w_glu, b_glu, attn_sinks, w_out, final_g):
    bp, seq, _ = x_prompt.shape
    bs = x_sample.shape[0]
    past_len = 8192
    assert seq % (CHUNK * 128) == 0 and x_sample.shape[1] == 1 and cache_win_k.shape[2] == WINDOW
    assert bp + bs <= MOD_ROWS

    w_in_b = jnp.concatenate(
        [w_in[..., 0:2048], w_in[..., 4096:6144], w_in[..., 7168:9216], w_in[..., 2048:4096],
         w_in[..., 6144:6656], w_in[..., 6656:7168]], axis=-1).astype(BF16)
    w_glu_b = w_glu.astype(BF16)
    w_out_b4 = w_out.astype(BF16).reshape(DEPTH, 2, D_SSM, D_MODEL)
    norm_g3 = norm_g.reshape(DEPTH, 1, D_MODEL)
    b_glu3 = b_glu.reshape(DEPTH, 1, D_SSM)
    ssm_d3 = ssm_d.reshape(DEPTH, 1, D_SSM)

    c_all = jnp.concatenate([c_prompt, c_sample, jnp.zeros((MOD_ROWS - bp - bs, D_MODEL), F32)], axis=0)
    mod = _modulation(c_all, w_ada, b_ada)
    mod_p = mod[:, :bp].reshape(DEPTH, bp, 1, 3 * D_MODEL)
    mod_s = mod[:, bp:bp + bs].reshape(DEPTH, 1, bs, 3 * D_MODEL)

    prep = _ssm_prep(ssm_a_re, ssm_a_im, ssm_log_dt, ssm_b_re, ssm_b_im, ssm_c_re, ssm_c_im)
    cos_p, sin_p = _rope_tables(jnp.arange(seq))
    cos_s, sin_s = _rope_tables(jnp.full((bs,), past_len))
    cache_k2 = cache_win_k.reshape(DEPTH, bs, WINDOW, D_KV)
    cache_v2 = cache_win_v.reshape(DEPTH, bs, WINDOW, D_KV)

    xp = x_prompt.reshape(bp * seq, D_MODEL)
    pk, pv, pr, pi = [], [], [], []
    for l in range(DEPTH):
        h = _rms_mod(xp.reshape(bp, seq, D_MODEL), norm_g3, mod_p, l, 256)
        u, zb = _in_proj(h.reshape(bp * seq, D_MODEL), w_in_b, cos_p, sin_p, l, 1024)
        prep_l = prep
        yg, hr, hi = _ssm_prompt(u.reshape(bp, seq, D_SSM), prep_l, ssm_d3, l)
        oa = _attention_prompt(zb.reshape(bp, seq, ZB_COLS), attn_sinks, l)
        ys = _glu(yg.reshape(bp * seq, D_SSM), zb, w_glu_b, b_glu3, l, 512)
        xp = _out_proj(ys, oa.reshape(bp * seq, D_ATT), w_out_b4, xp, mod_p, l, 512, seq)
        zb3 = zb.reshape(bp, seq, ZB_COLS)
        pk.append(zb3[:, seq - WINDOW:, ZB_K:ZB_K + D_KV].astype(F32).reshape(bp, WINDOW, N_KV_HEADS, HEAD_DIM))
        pv.append(zb3[:, seq - WINDOW:, ZB_V:ZB_V + D_KV].astype(F32).reshape(bp, WINDOW, N_KV_HEADS, HEAD_DIM))
        pr.append(hr.reshape(bp, N_GROUPS, SSM_STATE))
        pi.append(hi.reshape(bp, N_GROUPS, SSM_STATE))
    y_prompt = _final_norm(xp, final_g.reshape(1, D_MODEL), 256).reshape(bp, seq, D_MODEL)

    xs = x_sample.reshape(bs, D_MODEL)
    sk, sv, sr, si = [], [], [], []
    for l in range(DEPTH):
        h = _rms_mod(xs.reshape(1, bs, D_MODEL), norm_g3, mod_s, l, bs)
        u, zb = _in_proj(h.reshape(bs, D_MODEL), w_in_b, cos_s, sin_s, l, bs)
        ut = u.reshape(bs, N_GROUPS, SSM_GROUP).transpose(1, 2, 0)
        ht_r = state_ssm_re[l].transpose(1, 2, 0)
        ht_i = state_ssm_im[l].transpose(1, 2, 0)
        ygt, nhr, nhi = _ssm_step(ut, ht_r, ht_i, ssm_a_re, ssm_a_im, ssm_log_dt, ssm_b_re, ssm_b_im,
                                  ssm_c_re, ssm_c_im, ssm_d, l)
        yg = ygt.transpose(2, 0, 1).reshape(bs, D_SSM).astype(BF16)
        zf = zb.astype(F32)
        q4 = zf[:, ZB_Q:ZB_Q + D_ATT].reshape(bs, N_KV_HEADS, KV_GROUP, HEAD_DIM)
        ga4 = zf[:, ZB_GA:ZB_GA + D_ATT].reshape(bs, N_KV_HEADS, KV_GROUP, HEAD_DIM)
        kn = zf[:, ZB_K:ZB_K + D_KV].reshape(bs, N_KV_HEADS, 1, HEAD_DIM)
        vn = zf[:, ZB_V:ZB_V + D_KV].reshape(bs, N_KV_HEADS, 1, HEAD_DIM)
        oa = _attention_step(q4, ga4, kn, vn, cache_k2, cache_v2, attn_sinks, l)
        ys = _glu(yg, zb, w_glu_b, b_glu3, l, bs)
        xs = _out_proj(ys, oa.reshape(bs, D_ATT).astype(BF16), w_out_b4, xs, mod_s, l, bs, bs)
        sk.append(jnp.concatenate([cache_win_k[l][:, 1:], kn.reshape(bs, 1, N_KV_HEADS, HEAD_DIM)], axis=1))
        sv.append(jnp.concatenate([cache_win_v[l][:, 1:], vn.reshape(bs, 1, N_KV_HEADS, HEAD_DIM)], axis=1))
        sr.append(nhr.transpose(2, 0, 1))
        si.append(nhi.transpose(2, 0, 1))
    y_sample = _final_norm(xs, final_g.reshape(1, D_MODEL), bs).reshape(bs, 1, D_MODEL)

    return (y_prompt, y_sample, jnp.stack(pk), jnp.stack(pv), jnp.stack(pr), jnp.stack(pi),
            jnp.stack(sk), jnp.stack(sv), jnp.stack(sr), jnp.stack(si))
```

```python
import functools
import math

import jax
import jax.numpy as jnp
from jax import lax
from jax.experimental import pallas as pl
from jax.experimental.pallas import tpu as pltpu

F32 = jnp.float32
BF16 = jnp.bfloat16

D_MODEL = 4096
DEPTH = 4
D_SSM = 2048
D_ATT = 2048
SSM_GROUP = 16
N_GROUPS = 128
SSM_STATE = 64
HEAD_DIM = 128
N_HEADS = 16
N_KV_HEADS = 4
KV_GROUP = 4
D_KV = 512
WINDOW = 128
ROPE_THETA = 10000.0
EPS = 1e-6
BLOCK = 128
D_IN = 9216
ATT_SCALE = HEAD_DIM ** -0.5

IN_TN = 512
IN_TILES = D_IN // IN_TN
U_TILES = D_SSM // IN_TN
ZB_COLS = D_IN - D_SSM
ZB_Q, ZB_GA, ZB_GS, ZB_K, ZB_V = 0, 2048, 4096, 6144, 6656

CHUNK = 16
SSM_GB = 8
SSM_W = SSM_GB * SSM_GROUP
SCAN_LANES = 128
MOD_ROWS = 40

VMEM_LIMIT_BYTES = 56 * 1024 * 1024


def _params(*sem):
    return pltpu.CompilerParams(dimension_semantics=sem, vmem_limit_bytes=VMEM_LIMIT_BYTES)


def _mod_body(c_ref, w_ref, b_ref, o_ref):
    a = jax.nn.silu(c_ref[...]).astype(BF16)
    o_ref[...] = jnp.dot(a, w_ref[...].astype(BF16), preferred_element_type=F32) + b_ref[...]


def _modulation(c_all, w_ada, b_ada):
    tn = 512
    return pl.pallas_call(
        _mod_body,
        grid=(DEPTH, 3 * D_MODEL // tn),
        in_specs=[pl.BlockSpec((MOD_ROWS, D_MODEL), lambda l, j: (0, 0)),
                  pl.BlockSpec((None, D_MODEL, tn), lambda l, j: (l, 0, j)),
                  pl.BlockSpec((None, 1, tn), lambda l, j: (l, 0, j))],
        out_specs=pl.BlockSpec((None, MOD_ROWS, tn), lambda l, j: (l, 0, j)),
        out_shape=jax.ShapeDtypeStruct((DEPTH, MOD_ROWS, 3 * D_MODEL), F32),
        compiler_params=_params("arbitrary", "arbitrary"),
        name="adaln_mod",
    )(c_all, w_ada, b_ada.reshape(DEPTH, 1, 3 * D_MODEL))


def _rmsmod_body(x_ref, g_ref, sc_ref, sh_ref, o_ref):
    x = x_ref[...]
    y = x * lax.rsqrt(jnp.mean(x * x, axis=-1, keepdims=True) + EPS)
    o_ref[...] = ((y * g_ref[...]) * (1.0 + sc_ref[...]) + sh_ref[...]).astype(BF16)


def _rms_mod(x3, norm_g3, mod4, l, tm):
    bt, lr, _ = x3.shape
    r = mod4.shape[2]
    return pl.pallas_call(
        _rmsmod_body,
        grid=(bt, lr // tm),
        in_specs=[pl.BlockSpec((None, tm, D_MODEL), lambda b, i: (b, i, 0)),
                  pl.BlockSpec((None, 1, D_MODEL), lambda b, i: (l, 0, 0)),
                  pl.BlockSpec((None, None, r, D_MODEL), lambda b, i: (l, b, 0, 1)),
                  pl.BlockSpec((None, None, r, D_MODEL), lambda b, i: (l, b, 0, 0))],
        out_specs=pl.BlockSpec((None, tm, D_MODEL), lambda b, i: (b, i, 0)),
        out_shape=jax.ShapeDtypeStruct((bt, lr, D_MODEL), BF16),
        compiler_params=_params("arbitrary", "arbitrary"),
        name="rms_mod",
    )(x3, norm_g3, mod4, mod4)


def _rope(x, cos, sin):
    return x * cos + pltpu.roll(x, HEAD_DIM // 2, 1) * sin


def _inproj_body(h_ref, w_ref, cos_ref, sin_ref, u_ref, zb_ref):
    j = pl.program_id(1)
    half = IN_TN // 2

    def halves(epilogue):
        for c in range(2):
            sl = slice(c * half, (c + 1) * half)
            epilogue(sl, jnp.dot(h_ref[...], w_ref[:, sl], preferred_element_type=F32))

    @pl.when(j < 4)
    def _():
        def plain(sl, acc):
            u_ref[:, sl] = acc
        halves(plain)

    @pl.when(((j >= 4) & (j < 8)) | (j == 16))
    def _():
        def rope(sl, acc):
            for hh in range(half // HEAD_DIM):
                hs = slice(hh * HEAD_DIM, (hh + 1) * HEAD_DIM)
                zb_ref[:, sl.start + hs.start:sl.start + hs.stop] = _rope(
                    acc[:, hs], cos_ref[...], sin_ref[...]).astype(BF16)
        halves(rope)

    @pl.when((j >= 8) & (j < 16))
    def _():
        def silu(sl, acc):
            zb_ref[:, sl] = jax.nn.silu(acc).astype(BF16)
        halves(silu)

    @pl.when(j == 17)
    def _():
        def cast(sl, acc):
            zb_ref[:, sl] = acc.astype(BF16)
        halves(cast)


def _in_proj(h2, w_in_b, cos_t, sin_t, l, tm):
    m = h2.shape[0]
    nt = cos_t.shape[0] // tm
    return pl.pallas_call(
        _inproj_body,
        grid=(m // tm, IN_TILES),
        in_specs=[pl.BlockSpec((tm, D_MODEL), lambda i, j: (i, 0)),
                  pl.BlockSpec((None, D_MODEL, IN_TN), lambda i, j: (l, 0, j)),
                  pl.BlockSpec((tm, HEAD_DIM), lambda i, j: (i % nt, 0)),
                  pl.BlockSpec((tm, HEAD_DIM), lambda i, j: (i % nt, 0))],
        out_specs=[pl.BlockSpec((tm, IN_TN), lambda i, j: (i, jnp.minimum(j, U_TILES - 1))),
                   pl.BlockSpec((tm, IN_TN), lambda i, j: (i, jnp.maximum(j - U_TILES, 0)))],
        out_shape=[jax.ShapeDtypeStruct((m, D_SSM), F32),
                   jax.ShapeDtypeStruct((m, ZB_COLS), BF16)],
        compiler_params=_params("arbitrary", "arbitrary"),
        name="in_proj",
    )(h2, w_in_b, cos_t, sin_t)


def _cmul(ar, ai, br, bi):
    return ar * br - ai * bi, ar * bi + ai * br


def _discretise(ar, ai, log_dt):
    dt = jnp.exp(log_dt)
    mag = jnp.exp(ar * dt)
    lr = mag * jnp.cos(ai * dt)
    li = mag * jnp.sin(ai * dt)
    nr = lr - 1.0
    ni = li
    den = ar * ar + ai * ai
    qr = (nr * ar + ni * ai) / den
    qi = (ni * ar - nr * ai) / den
    return lr, li, qr, qi


def _powers(lr, li, n):
    pr, pi = [jnp.ones_like(lr)], [jnp.zeros_like(li)]
    for _ in range(n):
        r, i = _cmul(pr[-1], pi[-1], lr, li)
        pr.append(r)
        pi.append(i)
    return pr, pi


PREP_GB = 16


def _split_bf16(x):
    hi = x.astype(BF16)
    return hi, (x - hi.astype(F32)).astype(BF16)


def _ssm_prep_body(a2r_ref, a2i_ref, ldt_ref, cc_ref, cs_ref, bc_ref, bs_ref,
                   mt_ref, ws_ref, wc_ref, l16r_ref, l16i_ref, l1r_ref, l1i_ref, qr_ref, qi_ref):
    gb, t_len, p = PREP_GB, CHUNK, SSM_STATE
    kdim = t_len * SSM_GROUP
    lr, li, qr, qi = _discretise(a2r_ref[...], a2i_ref[...], ldt_ref[...])
    sgn = jnp.where(lax.broadcasted_iota(jnp.int32, (1, 2 * p), 1) < p, 1.0, -1.0)
    pr, pi = _powers(lr, li, t_len)
    mag2 = lr * lr + li * li
    ir, ii = _powers(lr / mag2, -li / mag2, t_len - 1)
    prs = [x * sgn for x in pr]
    w_inv = [_cmul(qr, qi, ir[s], ii[s]) for s in range(t_len)]
    w_15 = [_cmul(qr, qi, pr[t_len - 1 - s], pi[t_len - 1 - s]) for s in range(t_len)]
    w_inv = [(wr, wi * sgn) for wr, wi in w_inv]
    w_15 = [(wr, wi * sgn) for wr, wi in w_15]

    row_t = lax.broadcasted_iota(jnp.int32, (kdim, kdim), 0) >> 4
    col_s = lax.broadcasted_iota(jnp.int32, (kdim, kdim), 1) >> 4
    causal = row_t >= col_s
    nt = (((1,), (1,)), ((), ()))
    for g in range(gb):
        row = slice(g, g + 1)
        cc, cs = cc_ref[g], cs_ref[g]
        bc, bs = bc_ref[g], bs_ref[g]
        cl = jnp.concatenate([cc * prs[t][row] - cs * pi[t][row] for t in range(t_len)], axis=0)
        wc = jnp.concatenate([cc * prs[t + 1][row] - cs * pi[t + 1][row] for t in range(t_len)], axis=0)
        bl = jnp.concatenate([bc * w_inv[s][0][row] - bs * w_inv[s][1][row] for s in range(t_len)], axis=0)
        ws = jnp.concatenate([bc * w_15[s][0][row] - bs * w_15[s][1][row] for s in range(t_len)], axis=0)
        a_hi, a_lo = _split_bf16(cl)
        b_hi, b_lo = _split_bf16(bl)
        m = (lax.dot_general(a_hi, b_hi, nt, preferred_element_type=F32)
             + lax.dot_general(a_hi, b_lo, nt, preferred_element_type=F32)
             + lax.dot_general(a_lo, b_hi, nt, preferred_element_type=F32))
        mt_ref[g] = jnp.where(causal, m, 0.0).astype(BF16)
        ws_ref[g] = ws.T.astype(BF16)
        wc_ref[g] = wc.astype(BF16)

    pad = jnp.zeros((2 * p - gb, 2 * p), F32)
    outs = ((pr[t_len], l16r_ref), (pi[t_len], l16i_ref), (lr, l1r_ref), (li, l1i_ref),
            (qr, qr_ref), (qi, qi_ref))
    for val, ref in outs:
        col = jnp.concatenate([val, pad], axis=0).T
        for g in range(gb):
            ref[g] = jnp.broadcast_to(col[:p, g:g + 1], ref.shape[1:])


def _ssm_prep(a_re, a_im, log_dt, b_re, b_im, c_re, c_im):
    g, p, kdim, gb = N_GROUPS, SSM_STATE, CHUNK * SSM_GROUP, PREP_GB
    a2r = jnp.concatenate([a_re, a_re], axis=-1)
    a2i = jnp.concatenate([a_im, a_im], axis=-1)
    ldt = log_dt.reshape(DEPTH, g, 1)
    bt_re = jnp.swapaxes(b_re, -1, -2)
    bt_im = jnp.swapaxes(b_im, -1, -2)
    cc = jnp.concatenate([c_re, c_im], axis=-1)
    cs = jnp.concatenate([c_im, c_re], axis=-1)
    bc = jnp.concatenate([bt_re, bt_im], axis=-1)
    bs = jnp.concatenate([bt_im, bt_re], axis=-1)

    def spec(a, b):
        return pl.BlockSpec((None, gb, a, b), lambda l, i: (l, i, 0, 0))

    def spec2(b):
        return pl.BlockSpec((None, gb, b), lambda l, i: (l, i, 0))

    col_shape = jax.ShapeDtypeStruct((DEPTH, g, p, 1), F32)
    wide_shape = jax.ShapeDtypeStruct((DEPTH, g, p, SCAN_LANES), F32)
    return pl.pallas_call(
        _ssm_prep_body,
        grid=(DEPTH, g // gb),
        in_specs=[spec2(2 * p), spec2(2 * p), spec2(1),
                  spec(SSM_GROUP, 2 * p), spec(SSM_GROUP, 2 * p), spec(SSM_GROUP, 2 * p),
                  spec(SSM_GROUP, 2 * p)],
        out_specs=([spec(kdim, kdim), spec(2 * p, kdim), spec(kdim, 2 * p)]
                   + [spec(p, SCAN_LANES)] * 2 + [spec(p, 1)] * 4),
        out_shape=[jax.ShapeDtypeStruct((DEPTH, g, kdim, kdim), BF16),
                   jax.ShapeDtypeStruct((DEPTH, g, 2 * p, kdim), BF16),
                   jax.ShapeDtypeStruct((DEPTH, g, kdim, 2 * p), BF16)] + [wide_shape] * 2 + [col_shape] * 4,
        compiler_params=_params("arbitrary", "arbitrary"),
        name="ssm_prep",
    )(a2r, a2i, ldt, cc, cs, bc, bs)


def _ssm_body(u_ref, mt_ref, ws_ref, wc_ref, l16r_ref, l16i_ref, d_ref,
              yg_ref, hr_ref, hi_ref, ybuf):
    gb, t_len, p = SSM_GB, CHUNK, SSM_STATE
    r_len = u_ref.shape[0] // t_len
    slab_t = [u_ref[pl.ds(s, r_len, stride=t_len), :].T.astype(BF16) for s in range(t_len)]

    yl, xr, xi = [], [], []
    for g in range(gb):
        rows = slice(g * SSM_GROUP, (g + 1) * SSM_GROUP)
        s_g = jnp.concatenate([st[rows, :] for st in slab_t], axis=0)
        yl.append(jnp.dot(mt_ref[g], s_g, preferred_element_type=F32))
        x = jnp.dot(ws_ref[g], s_g, preferred_element_type=F32)
        xr.append(x[:p, :])
        xi.append(x[p:, :])
    h_r = jnp.concatenate(xr, axis=0)
    h_i = jnp.concatenate(xi, axis=0)

    a_r = l16r_ref[...].reshape(gb * p, r_len)
    a_i = l16i_ref[...].reshape(gb * p, r_len)
    lane = lax.broadcasted_iota(jnp.int32, (gb * p, r_len), 1)
    k = 1
    while k < r_len:
        keep = lane >= k
        s_r = jnp.where(keep, pltpu.roll(h_r, k, 1), 0.0)
        s_i = jnp.where(keep, pltpu.roll(h_i, k, 1), 0.0)
        h_r, h_i = h_r + a_r * s_r - a_i * s_i, h_i + a_r * s_i + a_i * s_r
        a_r, a_i = a_r * a_r - a_i * a_i, 2.0 * a_r * a_i
        k *= 2
    hr_ref[...] = h_r[:, r_len - 1:r_len].reshape(gb, p, 1)
    hi_ref[...] = h_i[:, r_len - 1:r_len].reshape(gb, p, 1)

    first = lane >= 1
    c_r = jnp.where(first, pltpu.roll(h_r, 1, 1), 0.0).astype(BF16)
    c_i = jnp.where(first, pltpu.roll(h_i, 1, 1), 0.0).astype(BF16)
    ys = []
    for g in range(gb):
        rows = slice(g * p, (g + 1) * p)
        c_g = jnp.concatenate([c_r[rows, :], c_i[rows, :]], axis=0)
        ys.append(yl[g] + jnp.dot(wc_ref[g], c_g, preferred_element_type=F32))
    d = d_ref[...]
    for t in range(t_len):
        rows = slice(t * SSM_GROUP, (t + 1) * SSM_GROUP)
        y_t = jnp.concatenate([y[rows, :] for y in ys], axis=0)
        u_t = u_ref[pl.ds(t, r_len, stride=t_len), :]
        ybuf[pl.ds(t, r_len, stride=t_len), :] = jax.nn.gelu(y_t.T + d * u_t)
    yg_ref[...] = ybuf[...].astype(BF16)


def _ssm_prompt(u3, prep, ssm_d3, l):
    b, seq, _ = u3.shape
    mt, ws, wc, l16r, l16i = prep[:5]
    gb, p, kdim, w = SSM_GB, SSM_STATE, CHUNK * SSM_GROUP, SSM_W

    def wspec(a, c):
        return pl.BlockSpec((None, gb, a, c), lambda bb, i: (l, i, 0, 0))

    st_spec = pl.BlockSpec((None, gb, p, 1), lambda bb, i: (bb, i, 0, 0))
    return pl.pallas_call(
        _ssm_body,
        grid=(b, N_GROUPS // gb),
        in_specs=[pl.BlockSpec((None, seq, w), lambda bb, i: (bb, 0, i)),
                  wspec(kdim, kdim), wspec(2 * p, kdim), wspec(kdim, 2 * p),
                  wspec(p, SCAN_LANES), wspec(p, SCAN_LANES),
                  pl.BlockSpec((None, 1, w), lambda bb, i: (l, 0, i))],
        out_specs=[pl.BlockSpec((None, seq, w), lambda bb, i: (bb, 0, i)), st_spec, st_spec],
        out_shape=[jax.ShapeDtypeStruct((b, seq, D_SSM), BF16),
                   jax.ShapeDtypeStruct((b, N_GROUPS, p, 1), F32),
                   jax.ShapeDtypeStruct((b, N_GROUPS, p, 1), F32)],
        scratch_shapes=[pltpu.VMEM((seq, w), F32)],
        compiler_params=_params("arbitrary", "arbitrary"),
        name="ssm_scan",
    )(u3, mt, ws, wc, l16r, l16i, ssm_d3)


SSM_STEP_GB = 16


def _ssm_step_body(lr_ref, li_ref, qr_ref, qi_ref, br_ref, bi_ref, cr_ref, ci_ref, d_ref, ut_ref,
                   hr_ref, hi_ref, yt_ref, nhr_ref, nhi_ref):
    lr, li = lr_ref[...], li_ref[...]
    bbr, bbi = _cmul(qr_ref[...], qi_ref[...], br_ref[...], bi_ref[...])
    for g in range(SSM_STEP_GB):
        u = ut_ref[g]
        ub = u.astype(BF16)
        x_r = jnp.dot(bbr[g].astype(BF16), ub, preferred_element_type=F32)
        x_i = jnp.dot(bbi[g].astype(BF16), ub, preferred_element_type=F32)
        h_r, h_i = hr_ref[g], hi_ref[g]
        n_r = lr[g] * h_r - li[g] * h_i + x_r
        n_i = lr[g] * h_i + li[g] * h_r + x_i
        y = (jnp.dot(cr_ref[g].astype(BF16), n_r.astype(BF16), preferred_element_type=F32)
             - jnp.dot(ci_ref[g].astype(BF16), n_i.astype(BF16), preferred_element_type=F32))
        yt_ref[g] = jax.nn.gelu(y + d_ref[g] * u)
        nhr_ref[g] = n_r
        nhi_ref[g] = n_i


def _ssm_step(ut, ht_r, ht_i, prep, b_re, b_im, c_re, c_im, ssm_d, l):
    l1r, l1i, qr, qi = prep[5:9]
    g, p, gb = N_GROUPS, SSM_STATE, SSM_STEP_GB
    bs = ut.shape[-1]

    def lspec(a, c):
        return pl.BlockSpec((None, gb, a, c), lambda i: (l, i, 0, 0))

    def aspec(a):
        return pl.BlockSpec((gb, a, bs), lambda i: (i, 0, 0))

    return pl.pallas_call(
        _ssm_step_body,
        grid=(g // gb,),
        in_specs=[lspec(p, 1), lspec(p, 1), lspec(p, 1), lspec(p, 1), lspec(p, SSM_GROUP), lspec(p, SSM_GROUP),
                  lspec(SSM_GROUP, p), lspec(SSM_GROUP, p), lspec(SSM_GROUP, 1),
                  aspec(SSM_GROUP), aspec(p), aspec(p)],
        out_specs=[aspec(SSM_GROUP), aspec(p), aspec(p)],
        out_shape=[jax.ShapeDtypeStruct((g, SSM_GROUP, bs), F32),
                   jax.ShapeDtypeStruct((g, p, bs), F32),
                   jax.ShapeDtypeStruct((g, p, bs), F32)],
        compiler_params=_params("arbitrary"),
        name="ssm_step",
    )(l1r, l1i, qr, qi, b_re, b_im, c_re, c_im, ssm_d.reshape(DEPTH, g, SSM_GROUP, 1), ut, ht_r, ht_i)


def _attn_body(l, q_ref, ga_ref, kc_ref, kp_ref, vc_ref, vp_ref, sink_ref, o_ref):
    n = pl.program_id(1)
    cols = KV_GROUP * BLOCK
    j = lax.broadcasted_iota(jnp.int32, (2 * BLOCK, cols), 0)
    r = lax.broadcasted_iota(jnp.int32, (2 * BLOCK, cols), 1) & (BLOCK - 1)
    no_prev = jnp.where(n > 0, 0, 2 * BLOCK)
    valid = ((j < BLOCK) & (j > r + no_prev)) | ((j >= BLOCK) & (j - BLOCK <= r))
    ones = jnp.ones((HEAD_DIM, 2 * BLOCK), BF16)
    nt = (((1,), (1,)), ((), ()))
    for kh in range(N_KV_HEADS):
        ksl = slice(kh * HEAD_DIM, (kh + 1) * HEAD_DIM)
        k = jnp.concatenate([kp_ref[:, ksl], kc_ref[:, ksl]], axis=0)
        vt = jnp.concatenate([vp_ref[:, ksl].T, vc_ref[:, ksl].T], axis=1)
        v1t = jnp.concatenate([vt, ones], axis=0)
        heads = [kh * KV_GROUP + g for g in range(KV_GROUP)]
        q4 = jnp.concatenate([q_ref[:, h * HEAD_DIM:(h + 1) * HEAD_DIM] for h in heads], axis=0)
        st = lax.dot_general(k, q4, nt, preferred_element_type=F32) * ATT_SCALE
        st = jnp.where(valid, st, -jnp.inf)
        sink = jnp.concatenate([jnp.full((1, BLOCK), sink_ref[l, h], F32) for h in heads], axis=1)
        m = jnp.maximum(jnp.max(st, axis=0, keepdims=True), sink)
        e = jnp.exp(st - m).astype(BF16)
        o1 = jnp.dot(v1t, e, preferred_element_type=F32)
        ot = o1[:HEAD_DIM, :] / (o1[HEAD_DIM:, :] + jnp.exp(sink - m))
        for g, h in enumerate(heads):
            hsl = slice(h * HEAD_DIM, (h + 1) * HEAD_DIM)
            o_ref[:, hsl] = (ot[:, g * BLOCK:(g + 1) * BLOCK].T * ga_ref[:, hsl].astype(F32)).astype(BF16)


def _attention_prompt(zb3, sinks, l):
    b, seq, _ = zb3.shape
    nb = seq // BLOCK
    wide = pl.BlockSpec((None, BLOCK, D_ATT), lambda bb, n: (bb, n, ZB_Q // D_ATT))
    gate = pl.BlockSpec((None, BLOCK, D_ATT), lambda bb, n: (bb, n, ZB_GA // D_ATT))

    def kv(col, prev):
        if prev:
            return pl.BlockSpec((None, BLOCK, D_KV), lambda bb, n: (bb, jnp.maximum(n - 1, 0), col // D_KV))
        return pl.BlockSpec((None, BLOCK, D_KV), lambda bb, n: (bb, n, col // D_KV))

    return pl.pallas_call(
        functools.partial(_attn_body, l),
        grid=(b, nb),
        in_specs=[wide, gate, kv(ZB_K, False), kv(ZB_K, True), kv(ZB_V, False), kv(ZB_V, True),
                  pl.BlockSpec(memory_space=pltpu.SMEM)],
        out_specs=pl.BlockSpec((None, BLOCK, D_ATT), lambda bb, n: (bb, n, 0)),
        out_shape=jax.ShapeDtypeStruct((b, seq, D_ATT), BF16),
        compiler_params=_params("arbitrary", "arbitrary"),
        name="swa_prompt",
    )(zb3, zb3, zb3, zb3, zb3, zb3, sinks)


STEP_SB = 8


def _attn_step_body(l, q_ref, ga_ref, kn_ref, vn_ref, ck_ref, cv_ref, sink_ref, o_ref):
    rows = WINDOW * N_KV_HEADS
    head = lax.broadcasted_iota(jnp.int32, (N_HEADS, rows), 0)
    col = lax.broadcasted_iota(jnp.int32, (N_HEADS, rows), 1)
    valid = ((col & (N_KV_HEADS - 1)) == (head >> 2)) & (col >= N_KV_HEADS)
    sink = jnp.concatenate([jnp.full((1, 1), sink_ref[l, h], F32) for h in range(N_HEADS)], axis=0)
    nt = (((1,), (1,)), ((), ()))

    def per_head(x4):
        return jnp.concatenate([jnp.broadcast_to(x4[kh:kh + 1, :], (KV_GROUP, HEAD_DIM))
                                for kh in range(N_KV_HEADS)], axis=0)

    for b in range(STEP_SB):
        q = q_ref[b].astype(BF16)
        kn = per_head(kn_ref[b]).astype(BF16).astype(F32)
        vn = per_head(vn_ref[b]).astype(BF16).astype(F32)
        s = lax.dot_general(q, ck_ref[b].astype(BF16), nt, preferred_element_type=F32) * ATT_SCALE
        s = jnp.where(valid, s, -jnp.inf)
        s_new = jnp.sum(q.astype(F32) * kn, axis=-1, keepdims=True) * ATT_SCALE
        m = jnp.maximum(jnp.maximum(jnp.max(s, axis=-1, keepdims=True), s_new), sink)
        e = jnp.exp(s - m)
        e_new = jnp.exp(s_new - m)
        den = jnp.sum(e, axis=-1, keepdims=True) + e_new + jnp.exp(sink - m)
        o = (jnp.dot((e / den).astype(BF16), cv_ref[b].astype(BF16), preferred_element_type=F32)
             + (e_new / den).astype(BF16).astype(F32) * vn)
        o_ref[b] = o * ga_ref[b]


def _attention_step(q3, ga3, kn3, vn3, cache_k, cache_v, sinks, l):
    bs = q3.shape[0]
    sb = STEP_SB
    rows = WINDOW * N_KV_HEADS
    hspec = pl.BlockSpec((sb, N_HEADS, HEAD_DIM), lambda i: (i, 0, 0))
    nspec = pl.BlockSpec((sb, N_KV_HEADS, HEAD_DIM), lambda i: (i, 0, 0))
    cspec = pl.BlockSpec((None, sb, rows, HEAD_DIM), lambda i: (l, i, 0, 0))
    ck = cache_k.reshape(DEPTH, bs, rows, HEAD_DIM)
    cv = cache_v.reshape(DEPTH, bs, rows, HEAD_DIM)
    return pl.pallas_call(
        functools.partial(_attn_step_body, l),
        grid=(bs // sb,),
        in_specs=[hspec, hspec, nspec, nspec, cspec, cspec, pl.BlockSpec(memory_space=pltpu.SMEM)],
        out_specs=hspec,
        out_shape=jax.ShapeDtypeStruct((bs, N_HEADS, HEAD_DIM), F32),
        compiler_params=_params("arbitrary"),
        name="swa_step",
    )(q3, ga3, kn3, vn3, ck, cv, sinks)


def _roll_cache_body(ck_ref, cv_ref, kn_ref, vn_ref, ok_ref, ov_ref):
    for c_ref, n_ref, o_ref in ((ck_ref, kn_ref, ok_ref), (cv_ref, vn_ref, ov_ref)):
        o_ref[:, :WINDOW - 1] = c_ref[:, 1:]
        o_ref[:, WINDOW - 1:] = n_ref[...]


def _roll_cache(cache_k, cache_v, k_new, v_new):
    bs = cache_k.shape[1]
    sb = STEP_SB
    cspec = pl.BlockSpec((None, sb, WINDOW, N_KV_HEADS, HEAD_DIM), lambda l, i: (l, i, 0, 0, 0))
    nspec = pl.BlockSpec((None, sb, 1, N_KV_HEADS, HEAD_DIM), lambda l, i: (l, i, 0, 0, 0))
    shape = jax.ShapeDtypeStruct(cache_k.shape, F32)
    return pl.pallas_call(
        _roll_cache_body,
        grid=(DEPTH, bs // sb),
        in_specs=[cspec, cspec, nspec, nspec],
        out_specs=[cspec, cspec],
        out_shape=[shape, shape],
        compiler_params=_params("arbitrary", "arbitrary"),
        name="roll_cache",
    )(cache_k, cache_v, k_new, v_new)


def _glu_body(y_ref, w_ref, b_ref, gs_ref, o_ref):
    y = y_ref[...]
    step = 512
    for c in range(D_SSM // step):
        sl = slice(c * step, (c + 1) * step)
        acc = jnp.dot(y, w_ref[:, sl], preferred_element_type=F32) + b_ref[:, sl]
        o_ref[:, sl] = (y_ref[:, sl].astype(F32) * jax.nn.sigmoid(acc)
                        * gs_ref[:, sl].astype(F32)).astype(BF16)


def _glu(yg2, zb2, w_glu_b, b_glu3, l, tm):
    m = yg2.shape[0]
    return pl.pallas_call(
        _glu_body,
        grid=(m // tm,),
        in_specs=[pl.BlockSpec((tm, D_SSM), lambda i: (i, 0)),
                  pl.BlockSpec((None, D_SSM, D_SSM), lambda i: (l, 0, 0)),
                  pl.BlockSpec((None, 1, D_SSM), lambda i: (l, 0, 0)),
                  pl.BlockSpec((tm, D_SSM), lambda i: (i, ZB_GS // D_SSM))],
        out_specs=pl.BlockSpec((tm, D_SSM), lambda i: (i, 0)),
        out_shape=jax.ShapeDtypeStruct((m, D_SSM), BF16),
        compiler_params=_params("arbitrary"),
        name="glu",
    )(yg2, w_glu_b, b_glu3, zb2)


def _outproj_body(a_ref, b_ref, w1_ref, w2_ref, x_ref, g_ref, o_ref):
    acc = (jnp.dot(a_ref[...], w1_ref[...], preferred_element_type=F32)
           + jnp.dot(b_ref[...], w2_ref[...], preferred_element_type=F32))
    o_ref[...] = x_ref[...] + g_ref[...] * acc


def _out_proj(ys2, oa2, w_out_b4, x2, mod4, l, tm, rows_per_batch):
    m = x2.shape[0]
    tn = 1024
    r = mod4.shape[2]
    per_b = rows_per_batch // tm
    gate_blk = 2 * D_MODEL // tn
    return pl.pallas_call(
        _outproj_body,
        grid=(D_MODEL // tn, m // tm),
        in_specs=[pl.BlockSpec((tm, D_SSM), lambda j, i: (i, 0)),
                  pl.BlockSpec((tm, D_ATT), lambda j, i: (i, 0)),
                  pl.BlockSpec((None, None, D_SSM, tn), lambda j, i: (l, 0, 0, j)),
                  pl.BlockSpec((None, None, D_ATT, tn), lambda j, i: (l, 1, 0, j)),
                  pl.BlockSpec((tm, tn), lambda j, i: (i, j)),
                  pl.BlockSpec((None, None, r, tn), lambda j, i: (l, i // per_b, 0, gate_blk + j))],
        out_specs=pl.BlockSpec((tm, tn), lambda j, i: (i, j)),
        out_shape=jax.ShapeDtypeStruct((m, D_MODEL), F32),
        compiler_params=_params("arbitrary", "arbitrary"),
        name="out_proj",
    )(ys2, oa2, w_out_b4, w_out_b4, x2, mod4)


def _final_norm_body(x_ref, g_ref, o_ref):
    x = x_ref[...]
    y = x * lax.rsqrt(jnp.mean(x * x, axis=-1, keepdims=True) + EPS)
    o_ref[...] = y * g_ref[...]


def _final_norm(x2, final_g2, tm):
    m = x2.shape[0]
    return pl.pallas_call(
        _final_norm_body,
        grid=(m // tm,),
        in_specs=[pl.BlockSpec((tm, D_MODEL), lambda i: (i, 0)),
                  pl.BlockSpec((1, D_MODEL), lambda i: (0, 0))],
        out_specs=pl.BlockSpec((tm, D_MODEL), lambda i: (i, 0)),
        out_shape=jax.ShapeDtypeStruct((m, D_MODEL), F32),
        compiler_params=_params("arbitrary"),
        name="final_norm",
    )(x2, final_g2)


def _rope_tables(pos):
    half = HEAD_DIM // 2
    inv = ROPE_THETA ** (-jnp.arange(half, dtype=F32) / half)
    ang = pos.astype(F32)[:, None] * inv[None, :]
    cos, sin = jnp.cos(ang), jnp.sin(ang)
    return jnp.concatenate([cos, cos], axis=-1), jnp.concatenate([-sin, sin], axis=-1)


def kernel(x_prompt, x_sample, c_prompt, c_sample, cache_win_k, cache_win_v, state_ssm_re, state_ssm_im, w_ada, b_ada, norm_g, w_in, ssm_a_re, ssm_a_im, ssm_log_dt, ssm_b_re, ssm_b_im, ssm_c_re, ssm_c_im, ssm_d, w_glu, b_glu, attn_sinks, w_out, final_g):
    bp, seq, _ = x_prompt.shape
    bs = x_sample.shape[0]
    past_len = 8192
    assert seq % (CHUNK * 128) == 0 and x_sample.shape[1] == 1 and cache_win_k.shape[2] == WINDOW
    assert bp + bs <= MOD_ROWS

    w_in_b = jnp.concatenate(
        [w_in[..., 0:2048], w_in[..., 4096:6144], w_in[..., 7168:9216], w_in[..., 2048:4096],
         w_in[..., 6144:6656], w_in[..., 6656:7168]], axis=-1).astype(BF16)
    w_glu_b = w_glu.astype(BF16)
    w_out_b4 = w_out.astype(BF16).reshape(DEPTH, 2, D_SSM, D_MODEL)
    norm_g3 = norm_g.reshape(DEPTH, 1, D_MODEL)
    b_glu3 = b_glu.reshape(DEPTH, 1, D_SSM)
    ssm_d3 = ssm_d.reshape(DEPTH, 1, D_SSM)

    c_all = jnp.concatenate([c_prompt, c_sample, jnp.zeros((MOD_ROWS - bp - bs, D_MODEL), F32)], axis=0)
    mod = _modulation(c_all, w_ada, b_ada)
    mod_p = mod[:, :bp].reshape(DEPTH, bp, 1, 3 * D_MODEL)
    mod_s = mod[:, bp:bp + bs].reshape(DEPTH, 1, bs, 3 * D_MODEL)

    prep = _ssm_prep(ssm_a_re, ssm_a_im, ssm_log_dt, ssm_b_re, ssm_b_im, ssm_c_re, ssm_c_im)
    cos_p, sin_p = _rope_tables(jnp.arange(seq))
    cos_s, sin_s = _rope_tables(jnp.full((bs,), past_len))

    xp = x_prompt.reshape(bp * seq, D_MODEL)
    pk, pv, pr, pi = [], [], [], []
    for l in range(DEPTH):
        h = _rms_mod(xp.reshape(bp, seq, D_MODEL), norm_g3, mod_p, l, 256)
        u, zb = _in_proj(h.reshape(bp * seq, D_MODEL), w_in_b, cos_p, sin_p, l, 1024)
        yg, hr, hi = _ssm_prompt(u.reshape(bp, seq, D_SSM), prep, ssm_d3, l)
        oa = _attention_prompt(zb.reshape(bp, seq, ZB_COLS), attn_sinks, l)
        ys = _glu(yg.reshape(bp * seq, D_SSM), zb, w_glu_b, b_glu3, l, 512)
        xp = _out_proj(ys, oa.reshape(bp * seq, D_ATT), w_out_b4, xp, mod_p, l, 512, seq)
        zb3 = zb.reshape(bp, seq, ZB_COLS)
        pk.append(zb3[:, seq - WINDOW:, ZB_K:ZB_K + D_KV].astype(F32).reshape(bp, WINDOW, N_KV_HEADS, HEAD_DIM))
        pv.append(zb3[:, seq - WINDOW:, ZB_V:ZB_V + D_KV].astype(F32).reshape(bp, WINDOW, N_KV_HEADS, HEAD_DIM))
        pr.append(hr.reshape(bp, N_GROUPS, SSM_STATE))
        pi.append(hi.reshape(bp, N_GROUPS, SSM_STATE))
    y_prompt = _final_norm(xp, final_g.reshape(1, D_MODEL), 256).reshape(bp, seq, D_MODEL)

    xs = x_sample.reshape(bs, D_MODEL)
    sk, sv, sr, si = [], [], [], []
    for l in range(DEPTH):
        h = _rms_mod(xs.reshape(1, bs, D_MODEL), norm_g3, mod_s, l, bs)
        u, zb = _in_proj(h.reshape(bs, D_MODEL), w_in_b, cos_s, sin_s, l, bs)
        ut = u.reshape(bs, N_GROUPS, SSM_GROUP).transpose(1, 2, 0)
        ht_r = state_ssm_re[l].transpose(1, 2, 0)
        ht_i = state_ssm_im[l].transpose(1, 2, 0)
        ygt, nhr, nhi = _ssm_step(ut, ht_r, ht_i, prep, ssm_b_re, ssm_b_im, ssm_c_re, ssm_c_im, ssm_d, l)
        yg = ygt.transpose(2, 0, 1).reshape(bs, D_SSM).astype(BF16)
        zf = zb.astype(F32)
        q3 = zf[:, ZB_Q:ZB_Q + D_ATT].reshape(bs, N_HEADS, HEAD_DIM)
        ga3 = zf[:, ZB_GA:ZB_GA + D_ATT].reshape(bs, N_HEADS, HEAD_DIM)
        kn = zf[:, ZB_K:ZB_K + D_KV].reshape(bs, N_KV_HEADS, HEAD_DIM)
        vn = zf[:, ZB_V:ZB_V + D_KV].reshape(bs, N_KV_HEADS, HEAD_DIM)
        oa = _attention_step(q3, ga3, kn, vn, cache_win_k, cache_win_v, attn_sinks, l)
        ys = _glu(yg, zb, w_glu_b, b_glu3, l, bs)
        xs = _out_proj(ys, oa.reshape(bs, D_ATT).astype(BF16), w_out_b4, xs, mod_s, l, bs, bs)
        sk.append(kn.reshape(bs, 1, N_KV_HEADS, HEAD_DIM))
        sv.append(vn.reshape(bs, 1, N_KV_HEADS, HEAD_DIM))
        sr.append(nhr.transpose(2, 0, 1))
        si.append(nhi.transpose(2, 0, 1))
    y_sample = _final_norm(xs, final_g.reshape(1, D_MODEL), bs).reshape(bs, 1, D_MODEL)
    new_k, new_v = _roll_cache(cache_win_k, cache_win_v, jnp.stack(sk), jnp.stack(sv))

    return (y_prompt, y_sample, jnp.stack(pk), jnp.stack(pv), jnp.stack(pr), jnp.stack(pi),
            new_k, new_v, jnp.stack(sr), jnp.stack(si))
```

```python
import functools
import math

import jax
import jax.numpy as jnp
from jax import lax
from jax.experimental import pallas as pl
from jax.experimental.pallas import tpu as pltpu

F32 = jnp.float32
BF16 = jnp.bfloat16

D_MODEL = 4096
DEPTH = 4
D_SSM = 2048
D_ATT = 2048
SSM_GROUP = 16
N_GROUPS = 128
SSM_STATE = 64
HEAD_DIM = 128
N_HEADS = 16
N_KV_HEADS = 4
KV_GROUP = 4
D_KV = 512
WINDOW = 128
ROPE_THETA = 10000.0
EPS = 1e-6
BLOCK = 128
D_IN = 9216
ATT_SCALE = HEAD_DIM ** -0.5

IN_TN = 512
IN_TILES = D_IN // IN_TN
U_TILES = D_SSM // IN_TN
ZB_COLS = D_IN - D_SSM
ZB_Q, ZB_GA, ZB_GS, ZB_K, ZB_V = 0, 2048, 4096, 6144, 6656

CHUNK = 16
SSM_GB = 8
SSM_W = SSM_GB * SSM_GROUP
SCAN_LANES = 128
MOD_ROWS = 40

VMEM_LIMIT_BYTES = 56 * 1024 * 1024


def _params(*sem):
    return pltpu.CompilerParams(dimension_semantics=sem, vmem_limit_bytes=VMEM_LIMIT_BYTES)


def _mod_body(c_ref, w_ref, b_ref, o_ref):
    a = jax.nn.silu(c_ref[...]).astype(BF16)
    o_ref[...] = jnp.dot(a, w_ref[...].astype(BF16), preferred_element_type=F32) + b_ref[...]


def _modulation(c_all, w_ada, b_ada):
    tn = 512
    return pl.pallas_call(
        _mod_body,
        grid=(DEPTH, 3 * D_MODEL // tn),
        in_specs=[pl.BlockSpec((MOD_ROWS, D_MODEL), lambda l, j: (0, 0)),
                  pl.BlockSpec((None, D_MODEL, tn), lambda l, j: (l, 0, j)),
                  pl.BlockSpec((None, 1, tn), lambda l, j: (l, 0, j))],
        out_specs=pl.BlockSpec((None, MOD_ROWS, tn), lambda l, j: (l, 0, j)),
        out_shape=jax.ShapeDtypeStruct((DEPTH, MOD_ROWS, 3 * D_MODEL), F32),
        compiler_params=_params("arbitrary", "arbitrary"),
        name="adaln_mod",
    )(c_all, w_ada, b_ada.reshape(DEPTH, 1, 3 * D_MODEL))


def _rmsmod_body(x_ref, g_ref, sc_ref, sh_ref, o_ref):
    x = x_ref[...]
    y = x * lax.rsqrt(jnp.mean(x * x, axis=-1, keepdims=True) + EPS)
    o_ref[...] = ((y * g_ref[...]) * (1.0 + sc_ref[...]) + sh_ref[...]).astype(BF16)


def _rms_mod(x3, kx, norm_g3, mod4, kb, l, tm):
    lr = x3.shape[1]
    r = mod4.shape[2]
    return pl.pallas_call(
        _rmsmod_body,
        grid=(lr // tm,),
        in_specs=[pl.BlockSpec((None, tm, D_MODEL), lambda i: (kx, i, 0)),
                  pl.BlockSpec((None, 1, D_MODEL), lambda i: (l, 0, 0)),
                  pl.BlockSpec((None, None, r, D_MODEL), lambda i: (l, kb, 0, 1)),
                  pl.BlockSpec((None, None, r, D_MODEL), lambda i: (l, kb, 0, 0))],
        out_specs=pl.BlockSpec((tm, D_MODEL), lambda i: (i, 0)),
        out_shape=jax.ShapeDtypeStruct((lr, D_MODEL), BF16),
        compiler_params=_params("arbitrary"),
        name="rms_mod",
    )(x3, norm_g3, mod4, mod4)


def _rope(x, cos, sin):
    return x * cos + pltpu.roll(x, HEAD_DIM // 2, 1) * sin


def _inproj_body(h_ref, w_ref, cos_ref, sin_ref, u_ref, zb_ref):
    j = pl.program_id(1)
    half = IN_TN // 2

    def halves(epilogue):
        for c in range(2):
            sl = slice(c * half, (c + 1) * half)
            epilogue(sl, jnp.dot(h_ref[...], w_ref[:, sl], preferred_element_type=F32))

    @pl.when(j < 4)
    def _():
        def plain(sl, acc):
            u_ref[:, sl] = acc
        halves(plain)

    @pl.when(((j >= 4) & (j < 8)) | (j == 16))
    def _():
        def rope(sl, acc):
            for hh in range(half // HEAD_DIM):
                hs = slice(hh * HEAD_DIM, (hh + 1) * HEAD_DIM)
                zb_ref[:, sl.start + hs.start:sl.start + hs.stop] = _rope(
                    acc[:, hs], cos_ref[...], sin_ref[...]).astype(BF16)
        halves(rope)

    @pl.when((j >= 8) & (j < 16))
    def _():
        def silu(sl, acc):
            zb_ref[:, sl] = jax.nn.silu(acc).astype(BF16)
        halves(silu)

    @pl.when(j == 17)
    def _():
        def cast(sl, acc):
            zb_ref[:, sl] = acc.astype(BF16)
        halves(cast)


def _in_proj(h2, w_in_b, cos_t, sin_t, l, tm):
    m = h2.shape[0]
    nt = cos_t.shape[0] // tm
    return pl.pallas_call(
        _inproj_body,
        grid=(m // tm, IN_TILES),
        in_specs=[pl.BlockSpec((tm, D_MODEL), lambda i, j: (i, 0)),
                  pl.BlockSpec((None, D_MODEL, IN_TN), lambda i, j: (l, 0, j)),
                  pl.BlockSpec((tm, HEAD_DIM), lambda i, j: (i % nt, 0)),
                  pl.BlockSpec((tm, HEAD_DIM), lambda i, j: (i % nt, 0))],
        out_specs=[pl.BlockSpec((tm, IN_TN), lambda i, j: (i, jnp.minimum(j, U_TILES - 1))),
                   pl.BlockSpec((tm, IN_TN), lambda i, j: (i, jnp.maximum(j - U_TILES, 0)))],
        out_shape=[jax.ShapeDtypeStruct((m, D_SSM), F32),
                   jax.ShapeDtypeStruct((m, ZB_COLS), BF16)],
        compiler_params=_params("arbitrary", "arbitrary"),
        name="in_proj",
    )(h2, w_in_b, cos_t, sin_t)


def _cmul(ar, ai, br, bi):
    return ar * br - ai * bi, ar * bi + ai * br


def _discretise(ar, ai, log_dt):
    dt = jnp.exp(log_dt)
    mag = jnp.exp(ar * dt)
    lr = mag * jnp.cos(ai * dt)
    li = mag * jnp.sin(ai * dt)
    nr = lr - 1.0
    ni = li
    den = ar * ar + ai * ai
    qr = (nr * ar + ni * ai) / den
    qi = (ni * ar - nr * ai) / den
    return lr, li, qr, qi


def _powers(lr, li, n):
    pr, pi = [jnp.ones_like(lr)], [jnp.zeros_like(li)]
    for _ in range(n):
        r, i = _cmul(pr[-1], pi[-1], lr, li)
        pr.append(r)
        pi.append(i)
    return pr, pi


PREP_GB = 16


def _split_bf16(x):
    hi = x.astype(BF16)
    return hi, (x - hi.astype(F32)).astype(BF16)


def _ssm_prep_body(a2r_ref, a2i_ref, ldt_ref, cc_ref, cs_ref, bc_ref, bs_ref,
                   mt_ref, ws_ref, wc_ref, l16r_ref, l16i_ref, l1r_ref, l1i_ref, qr_ref, qi_ref):
    gb, t_len, p = PREP_GB, CHUNK, SSM_STATE
    kdim = t_len * SSM_GROUP
    lr, li, qr, qi = _discretise(a2r_ref[...], a2i_ref[...], ldt_ref[...])
    sgn = jnp.where(lax.broadcasted_iota(jnp.int32, (1, 2 * p), 1) < p, 1.0, -1.0)
    pr, pi = _powers(lr, li, t_len)
    mag2 = lr * lr + li * li
    ir, ii = _powers(lr / mag2, -li / mag2, t_len - 1)
    prs = [x * sgn for x in pr]
    w_inv = [_cmul(qr, qi, ir[s], ii[s]) for s in range(t_len)]
    w_15 = [_cmul(qr, qi, pr[t_len - 1 - s], pi[t_len - 1 - s]) for s in range(t_len)]
    w_inv = [(wr, wi * sgn) for wr, wi in w_inv]
    w_15 = [(wr, wi * sgn) for wr, wi in w_15]

    row_t = lax.broadcasted_iota(jnp.int32, (kdim, kdim), 0) >> 4
    col_s = lax.broadcasted_iota(jnp.int32, (kdim, kdim), 1) >> 4
    causal = row_t >= col_s
    nt = (((1,), (1,)), ((), ()))
    for g in range(gb):
        row = slice(g, g + 1)
        cc, cs = cc_ref[g], cs_ref[g]
        bc, bs = bc_ref[g], bs_ref[g]
        cl = jnp.concatenate([cc * prs[t][row] - cs * pi[t][row] for t in range(t_len)], axis=0)
        wc = jnp.concatenate([cc * prs[t + 1][row] - cs * pi[t + 1][row] for t in range(t_len)], axis=0)
        bl = jnp.concatenate([bc * w_inv[s][0][row] - bs * w_inv[s][1][row] for s in range(t_len)], axis=0)
        ws = jnp.concatenate([bc * w_15[s][0][row] - bs * w_15[s][1][row] for s in range(t_len)], axis=0)
        a_hi, a_lo = _split_bf16(cl)
        b_hi, b_lo = _split_bf16(bl)
        m = (lax.dot_general(a_hi, b_hi, nt, preferred_element_type=F32)
             + lax.dot_general(a_hi, b_lo, nt, preferred_element_type=F32)
             + lax.dot_general(a_lo, b_hi, nt, preferred_element_type=F32))
        mt_ref[g] = jnp.where(causal, m, 0.0).astype(BF16)
        ws_ref[g] = ws.T.astype(BF16)
        wc_ref[g] = wc.astype(BF16)

    pad = jnp.zeros((2 * p - gb, 2 * p), F32)
    outs = ((pr[t_len], l16r_ref), (pi[t_len], l16i_ref), (lr, l1r_ref), (li, l1i_ref),
            (qr, qr_ref), (qi, qi_ref))
    for val, ref in outs:
        col = jnp.concatenate([val, pad], axis=0).T
        for g in range(gb):
            ref[g] = jnp.broadcast_to(col[:p, g:g + 1], ref.shape[1:])


def _ssm_prep(a_re, a_im, log_dt, b_re, b_im, c_re, c_im):
    g, p, kdim, gb = N_GROUPS, SSM_STATE, CHUNK * SSM_GROUP, PREP_GB
    a2r = jnp.concatenate([a_re, a_re], axis=-1)
    a2i = jnp.concatenate([a_im, a_im], axis=-1)
    ldt = log_dt.reshape(DEPTH, g, 1)
    bt_re = jnp.swapaxes(b_re, -1, -2)
    bt_im = jnp.swapaxes(b_im, -1, -2)
    cc = jnp.concatenate([c_re, c_im], axis=-1)
    cs = jnp.concatenate([c_im, c_re], axis=-1)
    bc = jnp.concatenate([bt_re, bt_im], axis=-1)
    bs = jnp.concatenate([bt_im, bt_re], axis=-1)

    def spec(a, b):
        return pl.BlockSpec((None, gb, a, b), lambda l, i: (l, i, 0, 0))

    def spec2(b):
        return pl.BlockSpec((None, gb, b), lambda l, i: (l, i, 0))

    col_shape = jax.ShapeDtypeStruct((DEPTH, g, p, 1), F32)
    wide_shape = jax.ShapeDtypeStruct((DEPTH, g, p, SCAN_LANES), F32)
    return pl.pallas_call(
        _ssm_prep_body,
        grid=(DEPTH, g // gb),
        in_specs=[spec2(2 * p), spec2(2 * p), spec2(1),
                  spec(SSM_GROUP, 2 * p), spec(SSM_GROUP, 2 * p), spec(SSM_GROUP, 2 * p),
                  spec(SSM_GROUP, 2 * p)],
        out_specs=([spec(kdim, kdim), spec(2 * p, kdim), spec(kdim, 2 * p)]
                   + [spec(p, SCAN_LANES)] * 2 + [spec(p, 1)] * 4),
        out_shape=[jax.ShapeDtypeStruct((DEPTH, g, kdim, kdim), BF16),
                   jax.ShapeDtypeStruct((DEPTH, g, 2 * p, kdim), BF16),
                   jax.ShapeDtypeStruct((DEPTH, g, kdim, 2 * p), BF16)] + [wide_shape] * 2 + [col_shape] * 4,
        compiler_params=_params("arbitrary", "arbitrary"),
        name="ssm_prep",
    )(a2r, a2i, ldt, cc, cs, bc, bs)


def _ssm_part(u_ref, mt_ref, ws_ref, wc_ref, l16r_ref, l16i_ref, d_ref,
              yg_ref, hr_ref, hi_ref, ubuf, ybuf):
    gb, t_len, p = SSM_GB, CHUNK, SSM_STATE
    r_len = u_ref.shape[0] // t_len
    ubuf[...] = u_ref[...].astype(F32)
    slab_t = [ubuf[pl.ds(s, r_len, stride=t_len), :].T.astype(BF16) for s in range(t_len)]

    yl, xr, xi = [], [], []
    for g in range(gb):
        rows = slice(g * SSM_GROUP, (g + 1) * SSM_GROUP)
        s_g = jnp.concatenate([st[rows, :] for st in slab_t], axis=0)
        both = jnp.dot(jnp.concatenate([mt_ref[g], ws_ref[g]], axis=0), s_g, preferred_element_type=F32)
        kd = t_len * SSM_GROUP
        yl.append(both[:kd, :])
        xr.append(both[kd:kd + p, :])
        xi.append(both[kd + p:, :])
    h_r = jnp.concatenate(xr, axis=0)
    h_i = jnp.concatenate(xi, axis=0)
    yield

    a_r = l16r_ref[...].reshape(gb * p, r_len)
    a_i = l16i_ref[...].reshape(gb * p, r_len)
    lane = lax.broadcasted_iota(jnp.int32, (gb * p, r_len), 1)
    k = 1
    while k < r_len:
        keep = lane >= k
        s_r = jnp.where(keep, pltpu.roll(h_r, k, 1), 0.0)
        s_i = jnp.where(keep, pltpu.roll(h_i, k, 1), 0.0)
        h_r, h_i = h_r + a_r * s_r - a_i * s_i, h_i + a_r * s_i + a_i * s_r
        a_r, a_i = a_r * a_r - a_i * a_i, 2.0 * a_r * a_i
        k *= 2
    hr_ref[...] = h_r[:, r_len - 1:r_len].reshape(gb, p, 1)
    hi_ref[...] = h_i[:, r_len - 1:r_len].reshape(gb, p, 1)

    first = lane >= 1
    c_r = jnp.where(first, pltpu.roll(h_r, 1, 1), 0.0).astype(BF16)
    c_i = jnp.where(first, pltpu.roll(h_i, 1, 1), 0.0).astype(BF16)
    ys = []
    for g in range(gb):
        rows = slice(g * p, (g + 1) * p)
        c_g = jnp.concatenate([c_r[rows, :], c_i[rows, :]], axis=0)
        ys.append(yl[g] + jnp.dot(wc_ref[g], c_g, preferred_element_type=F32))
    yield
    d = d_ref[...]
    for t in range(t_len):
        rows = slice(t * SSM_GROUP, (t + 1) * SSM_GROUP)
        y_t = jnp.concatenate([y[rows, :] for y in ys], axis=0)
        u_t = ubuf[pl.ds(t, r_len, stride=t_len), :]
        ybuf[pl.ds(t, r_len, stride=t_len), :] = jax.nn.gelu(y_t.T + d * u_t)
    yg_ref[...] = ybuf[...].astype(BF16)


SSM_STEP_GB = 16


def _ssm_step_body(lr_ref, li_ref, qr_ref, qi_ref, br_ref, bi_ref, cr_ref, ci_ref, d_ref, ut_ref,
                   hr_ref, hi_ref, yt_ref, nhr_ref, nhi_ref):
    lr, li = lr_ref[...], li_ref[...]
    bbr, bbi = _cmul(qr_ref[...], qi_ref[...], br_ref[...], bi_ref[...])
    for g in range(SSM_STEP_GB):
        u = ut_ref[g]
        ub = u.astype(BF16)
        x_r = jnp.dot(bbr[g].astype(BF16), ub, preferred_element_type=F32)
        x_i = jnp.dot(bbi[g].astype(BF16), ub, preferred_element_type=F32)
        h_r, h_i = hr_ref[g], hi_ref[g]
        n_r = lr[g] * h_r - li[g] * h_i + x_r
        n_i = lr[g] * h_i + li[g] * h_r + x_i
        y = (jnp.dot(cr_ref[g].astype(BF16), n_r.astype(BF16), preferred_element_type=F32)
             - jnp.dot(ci_ref[g].astype(BF16), n_i.astype(BF16), preferred_element_type=F32))
        yt_ref[g] = jax.nn.gelu(y + d_ref[g] * u)
        nhr_ref[g] = n_r
        nhi_ref[g] = n_i


def _ssm_step(ut, ht_r, ht_i, prep, b_re, b_im, c_re, c_im, ssm_d, l):
    l1r, l1i, qr, qi = prep[5:9]
    g, p, gb = N_GROUPS, SSM_STATE, SSM_STEP_GB
    bs = ut.shape[-1]

    def lspec(a, c):
        return pl.BlockSpec((None, gb, a, c), lambda i: (l, i, 0, 0))

    def aspec(a):
        return pl.BlockSpec((gb, a, bs), lambda i: (i, 0, 0))

    return pl.pallas_call(
        _ssm_step_body,
        grid=(g // gb,),
        in_specs=[lspec(p, 1), lspec(p, 1), lspec(p, 1), lspec(p, 1), lspec(p, SSM_GROUP), lspec(p, SSM_GROUP),
                  lspec(SSM_GROUP, p), lspec(SSM_GROUP, p), lspec(SSM_GROUP, 1),
                  aspec(SSM_GROUP), aspec(p), aspec(p)],
        out_specs=[aspec(SSM_GROUP), aspec(p), aspec(p)],
        out_shape=[jax.ShapeDtypeStruct((g, SSM_GROUP, bs), F32),
                   jax.ShapeDtypeStruct((g, p, bs), F32),
                   jax.ShapeDtypeStruct((g, p, bs), F32)],
        compiler_params=_params("arbitrary"),
        name="ssm_step",
    )(l1r, l1i, qr, qi, b_re, b_im, c_re, c_im, ssm_d.reshape(DEPTH, g, SSM_GROUP, 1), ut, ht_r, ht_i)


def _attn_part(l, n, q_ref, ga_ref, kc_ref, kp_ref, vc_ref, vp_ref, sink_ref, o_ref):
    cols = KV_GROUP * BLOCK
    j = lax.broadcasted_iota(jnp.int32, (2 * BLOCK, cols), 0)
    r = lax.broadcasted_iota(jnp.int32, (2 * BLOCK, cols), 1) & (BLOCK - 1)
    no_prev = jnp.where(n > 0, 0, 2 * BLOCK)
    valid = ((j < BLOCK) & (j > r + no_prev)) | ((j >= BLOCK) & (j - BLOCK <= r))
    ones = jnp.ones((HEAD_DIM, 2 * BLOCK), BF16)
    nt = (((1,), (1,)), ((), ()))
    heads = [[kh * KV_GROUP + g for g in range(KV_GROUP)] for kh in range(N_KV_HEADS)]
    scores = []
    for kh in range(N_KV_HEADS):
        ksl = slice(kh * HEAD_DIM, (kh + 1) * HEAD_DIM)
        k = jnp.concatenate([kp_ref[:, ksl], kc_ref[:, ksl]], axis=0)
        q4 = jnp.concatenate([q_ref[:, h * HEAD_DIM:(h + 1) * HEAD_DIM] for h in heads[kh]], axis=0)
        scores.append(lax.dot_general(k, q4, nt, preferred_element_type=F32))
    yield
    outs = []
    for kh in range(N_KV_HEADS):
        ksl = slice(kh * HEAD_DIM, (kh + 1) * HEAD_DIM)
        vt = jnp.concatenate([vp_ref[:, ksl].T, vc_ref[:, ksl].T], axis=1)
        v1t = jnp.concatenate([vt, ones], axis=0)
        st = jnp.where(valid, scores[kh] * ATT_SCALE, -jnp.inf)
        sink = jnp.concatenate([jnp.full((1, BLOCK), sink_ref[l, h], F32) for h in heads[kh]], axis=1)
        m = jnp.maximum(jnp.max(st, axis=0, keepdims=True), sink)
        e = jnp.exp(st - m).astype(BF16)
        o1 = jnp.dot(v1t, e, preferred_element_type=F32)
        outs.append((o1, jnp.exp(sink - m)))
    yield
    for kh in range(N_KV_HEADS):
        o1, e_sink = outs[kh]
        ot = o1[:HEAD_DIM, :] / (o1[HEAD_DIM:, :] + e_sink)
        for g, h in enumerate(heads[kh]):
            hsl = slice(h * HEAD_DIM, (h + 1) * HEAD_DIM)
            o_ref[:, hsl] = (ot[:, g * BLOCK:(g + 1) * BLOCK].T * ga_ref[:, hsl].astype(F32)).astype(BF16)


Z_Q, Z_GA, Z_GS, Z_K, Z_V = 2048, 4096, 6144, 8192, 8704
SA_STEPS = N_GROUPS // SSM_GB
IN_ROWS = 512


def _in_part(j, h_ref, w_ref, cos_ref, sin_ref, z_ref):
    is_rope = ((j >= 4) & (j < 8)) | (j == 16)
    is_silu = (j >= 8) & (j < 16)
    for rc in range(h_ref.shape[0] // IN_ROWS):
        rs = slice(rc * IN_ROWS, (rc + 1) * IN_ROWS)
        acc = jnp.dot(h_ref[rs, :], w_ref[...], preferred_element_type=F32)
        cos, sin = cos_ref[rs, :], sin_ref[rs, :]
        roped = jnp.concatenate([_rope(acc[:, hh * HEAD_DIM:(hh + 1) * HEAD_DIM], cos, sin)
                                 for hh in range(IN_TN // HEAD_DIM)], axis=1)
        out = jnp.where(is_rope, roped, jnp.where(is_silu, jax.nn.silu(acc), acc))
        z_ref[rs, :] = out.astype(BF16)
        yield


def _stage_body(has_in, has_sa, l_prev, *refs):
    j = pl.program_id(0)
    refs = list(refs)
    in_refs = [refs.pop(0) for _ in range(4)] if has_in else []
    sa_refs = [refs.pop(0) for _ in range(14)] if has_sa else []
    proj = _in_part(j, *in_refs, refs.pop(0)) if has_in else iter(())
    side = []
    if has_sa:
        u_ref, q_ref, ga_ref, kc_ref, kp_ref, vc_ref, vp_ref = sa_refs[:7]
        mt_ref, ws_ref, wc_ref, l16r_ref, l16i_ref, d_ref, sink_ref = sa_refs[7:]
        yg_ref, hr_ref, hi_ref, oa_ref, ubuf, ybuf = refs
        side = [_ssm_part(u_ref, mt_ref, ws_ref, wc_ref, l16r_ref, l16i_ref, d_ref, yg_ref, hr_ref,
                          hi_ref, ubuf, ybuf),
                _attn_part(l_prev, jnp.minimum(j, SA_STEPS - 1), q_ref, ga_ref, kc_ref, kp_ref, vc_ref,
                           vp_ref, sink_ref, oa_ref)]
    live = True
    while live:
        live = False
        for gen in side + [proj]:
            if next(gen, "done") != "done":
                live = True


def _prompt_stage(h, l_in, w_in_b, cos_t, sin_t, z_prev, l_prev, prep, ssm_d3, sinks):
    has_in, has_sa = h is not None, z_prev is not None
    seq = (h if has_in else z_prev).shape[0]
    gb, p, kdim = SSM_GB, SSM_STATE, CHUNK * SSM_GROUP
    in_specs, args, out_specs, out_shape, scratch = [], [], [], [], []

    def jm(j):
        return jnp.minimum(j, SA_STEPS - 1)

    if has_in:
        table = pl.BlockSpec((seq, HEAD_DIM), lambda j: (0, 0))
        in_specs += [pl.BlockSpec((seq, D_MODEL), lambda j: (0, 0), pipeline_mode=pl.Buffered(1)),
                     pl.BlockSpec((None, D_MODEL, IN_TN), lambda j: (l_in, 0, j)), table, table]
        args += [h, w_in_b, cos_t, sin_t]
        out_specs.append(pl.BlockSpec((seq, IN_TN), lambda j: (0, j)))
        out_shape.append(jax.ShapeDtypeStruct((seq, D_IN), BF16))
    if has_sa:
        def wspec(a, c):
            return pl.BlockSpec((None, gb, a, c), lambda j: (l_prev, jm(j), 0, 0))

        def kv(col, prev):
            if prev:
                return pl.BlockSpec((BLOCK, D_KV), lambda j: (jnp.maximum(jm(j) - 1, 0), col // D_KV))
            return pl.BlockSpec((BLOCK, D_KV), lambda j: (jm(j), col // D_KV))

        st_spec = pl.BlockSpec((gb, p, 1), lambda j: (jm(j), 0, 0))
        in_specs += [pl.BlockSpec((seq, SSM_W), lambda j: (0, jm(j))),
                     pl.BlockSpec((BLOCK, D_ATT), lambda j: (jm(j), Z_Q // D_ATT)),
                     pl.BlockSpec((BLOCK, D_ATT), lambda j: (jm(j), Z_GA // D_ATT)),
                     kv(Z_K, False), kv(Z_K, True), kv(Z_V, False), kv(Z_V, True),
                     wspec(kdim, kdim), wspec(2 * p, kdim), wspec(kdim, 2 * p),
                     wspec(p, SCAN_LANES), wspec(p, SCAN_LANES),
                     pl.BlockSpec((None, 1, SSM_W), lambda j: (l_prev, 0, jm(j))),
                     pl.BlockSpec(memory_space=pltpu.SMEM)]
        args += [z_prev] * 7 + list(prep[:5]) + [ssm_d3, sinks]
        out_specs += [pl.BlockSpec((seq, SSM_W), lambda j: (0, jm(j))), st_spec, st_spec,
                      pl.BlockSpec((BLOCK, D_ATT), lambda j: (jm(j), 0))]
        out_shape += [jax.ShapeDtypeStruct((seq, D_SSM), BF16),
                      jax.ShapeDtypeStruct((N_GROUPS, p, 1), F32),
                      jax.ShapeDtypeStruct((N_GROUPS, p, 1), F32),
                      jax.ShapeDtypeStruct((seq, D_ATT), BF16)]
        scratch = [pltpu.VMEM((seq, SSM_W), F32), pltpu.VMEM((seq, SSM_W), F32)]
    return pl.pallas_call(
        functools.partial(_stage_body, has_in, has_sa, l_prev),
        grid=(IN_TILES if has_in else SA_STEPS,),
        in_specs=in_specs, out_specs=out_specs, out_shape=out_shape, scratch_shapes=scratch,
        compiler_params=_params("arbitrary"),
        name="stage" + ("_in" if has_in else "") + ("_sa" if has_sa else ""),
    )(*args)


STEP_SB = 8


def _attn_step_body(l, q_ref, ga_ref, kn_ref, vn_ref, ck_ref, cv_ref, sink_ref, o_ref):
    rows = WINDOW * N_KV_HEADS
    head = lax.broadcasted_iota(jnp.int32, (N_HEADS, rows), 0)
    col = lax.broadcasted_iota(jnp.int32, (N_HEADS, rows), 1)
    valid = ((col & (N_KV_HEADS - 1)) == (head >> 2)) & (col >= N_KV_HEADS)
    sink = jnp.concatenate([jnp.full((1, 1), sink_ref[l, h], F32) for h in range(N_HEADS)], axis=0)
    nt = (((1,), (1,)), ((), ()))

    def per_head(x4):
        return jnp.concatenate([jnp.broadcast_to(x4[kh:kh + 1, :], (KV_GROUP, HEAD_DIM))
                                for kh in range(N_KV_HEADS)], axis=0)

    for b in range(STEP_SB):
        q = q_ref[b].astype(BF16)
        kn = per_head(kn_ref[b]).astype(BF16).astype(F32)
        vn = per_head(vn_ref[b]).astype(BF16).astype(F32)
        s = lax.dot_general(q, ck_ref[b].astype(BF16), nt, preferred_element_type=F32) * ATT_SCALE
        s = jnp.where(valid, s, -jnp.inf)
        s_new = jnp.sum(q.astype(F32) * kn, axis=-1, keepdims=True) * ATT_SCALE
        m = jnp.maximum(jnp.maximum(jnp.max(s, axis=-1, keepdims=True), s_new), sink)
        e = jnp.exp(s - m)
        e_new = jnp.exp(s_new - m)
        den = jnp.sum(e, axis=-1, keepdims=True) + e_new + jnp.exp(sink - m)
        o = (jnp.dot((e / den).astype(BF16), cv_ref[b].astype(BF16), preferred_element_type=F32)
             + (e_new / den).astype(BF16).astype(F32) * vn)
        o_ref[b] = o * ga_ref[b]


def _attention_step(q3, ga3, kn3, vn3, cache_k, cache_v, sinks, l):
    bs = q3.shape[0]
    sb = STEP_SB
    rows = WINDOW * N_KV_HEADS
    hspec = pl.BlockSpec((sb, N_HEADS, HEAD_DIM), lambda i: (i, 0, 0))
    nspec = pl.BlockSpec((sb, N_KV_HEADS, HEAD_DIM), lambda i: (i, 0, 0))
    cspec = pl.BlockSpec((None, sb, rows, HEAD_DIM), lambda i: (l, i, 0, 0))
    ck = cache_k.reshape(DEPTH, bs, rows, HEAD_DIM)
    cv = cache_v.reshape(DEPTH, bs, rows, HEAD_DIM)
    return pl.pallas_call(
        functools.partial(_attn_step_body, l),
        grid=(bs // sb,),
        in_specs=[hspec, hspec, nspec, nspec, cspec, cspec, pl.BlockSpec(memory_space=pltpu.SMEM)],
        out_specs=hspec,
        out_shape=jax.ShapeDtypeStruct((bs, N_HEADS, HEAD_DIM), F32),
        compiler_params=_params("arbitrary"),
        name="swa_step",
    )(q3, ga3, kn3, vn3, ck, cv, sinks)


def _roll_cache_body(ck_ref, cv_ref, kn_ref, vn_ref, ok_ref, ov_ref):
    for c_ref, n_ref, o_ref in ((ck_ref, kn_ref, ok_ref), (cv_ref, vn_ref, ov_ref)):
        o_ref[:, :WINDOW - 1] = c_ref[:, 1:]
        o_ref[:, WINDOW - 1:] = n_ref[...]


def _roll_cache(cache_k, cache_v, k_new, v_new):
    bs = cache_k.shape[1]
    sb = STEP_SB
    cspec = pl.BlockSpec((None, sb, WINDOW, N_KV_HEADS, HEAD_DIM), lambda l, i: (l, i, 0, 0, 0))
    nspec = pl.BlockSpec((None, sb, 1, N_KV_HEADS, HEAD_DIM), lambda l, i: (l, i, 0, 0, 0))
    shape = jax.ShapeDtypeStruct(cache_k.shape, F32)
    return pl.pallas_call(
        _roll_cache_body,
        grid=(DEPTH, bs // sb),
        in_specs=[cspec, cspec, nspec, nspec],
        out_specs=[cspec, cspec],
        out_shape=[shape, shape],
        compiler_params=_params("arbitrary", "arbitrary"),
        name="roll_cache",
    )(cache_k, cache_v, k_new, v_new)


def _glu_body(y_ref, w_ref, b_ref, gs_ref, o_ref):
    y = y_ref[...]
    step = 512
    for c in range(D_SSM // step):
        sl = slice(c * step, (c + 1) * step)
        acc = jnp.dot(y, w_ref[:, sl], preferred_element_type=F32) + b_ref[:, sl]
        o_ref[:, sl] = (y_ref[:, sl].astype(F32) * jax.nn.sigmoid(acc)
                        * gs_ref[:, sl].astype(F32)).astype(BF16)


def _glu(yg2, zb2, gs_col, w_glu_b, b_glu3, l, tm):
    m = yg2.shape[0]
    return pl.pallas_call(
        _glu_body,
        grid=(m // tm,),
        in_specs=[pl.BlockSpec((tm, D_SSM), lambda i: (i, 0)),
                  pl.BlockSpec((None, D_SSM, D_SSM), lambda i: (l, 0, 0)),
                  pl.BlockSpec((None, 1, D_SSM), lambda i: (l, 0, 0)),
                  pl.BlockSpec((tm, D_SSM), lambda i: (i, gs_col // D_SSM))],
        out_specs=pl.BlockSpec((tm, D_SSM), lambda i: (i, 0)),
        out_shape=jax.ShapeDtypeStruct((m, D_SSM), BF16),
        compiler_params=_params("arbitrary"),
        name="glu",
    )(yg2, w_glu_b, b_glu3, zb2)


def _outproj_body(a_ref, b_ref, w1_ref, w2_ref, x_ref, g_ref, o_ref):
    acc = (jnp.dot(a_ref[...], w1_ref[...], preferred_element_type=F32)
           + jnp.dot(b_ref[...], w2_ref[...], preferred_element_type=F32))
    o_ref[...] = x_ref[...] + g_ref[...] * acc


def _out_proj(ys2, oa2, w_out_b4, x3, kx, mod4, kb, l, tm):
    m = x3.shape[1]
    tn = 1024
    r = mod4.shape[2]
    gate_blk = 2 * D_MODEL // tn
    return pl.pallas_call(
        _outproj_body,
        grid=(D_MODEL // tn, m // tm),
        in_specs=[pl.BlockSpec((tm, D_SSM), lambda j, i: (i, 0)),
                  pl.BlockSpec((tm, D_ATT), lambda j, i: (i, 0)),
                  pl.BlockSpec((None, None, D_SSM, tn), lambda j, i: (l, 0, 0, j)),
                  pl.BlockSpec((None, None, D_ATT, tn), lambda j, i: (l, 1, 0, j)),
                  pl.BlockSpec((None, tm, tn), lambda j, i: (kx, i, j)),
                  pl.BlockSpec((None, None, r, tn), lambda j, i: (l, kb, 0, gate_blk + j))],
        out_specs=pl.BlockSpec((tm, tn), lambda j, i: (i, j)),
        out_shape=jax.ShapeDtypeStruct((m, D_MODEL), F32),
        compiler_params=_params("arbitrary", "arbitrary"),
        name="out_proj",
    )(ys2, oa2, w_out_b4, w_out_b4, x3, mod4)


def _final_norm_body(x_ref, g_ref, o_ref):
    x = x_ref[...]
    y = x * lax.rsqrt(jnp.mean(x * x, axis=-1, keepdims=True) + EPS)
    o_ref[...] = y * g_ref[...]


def _final_norm(x2, final_g2, tm):
    m = x2.shape[0]
    return pl.pallas_call(
        _final_norm_body,
        grid=(m // tm,),
        in_specs=[pl.BlockSpec((tm, D_MODEL), lambda i: (i, 0)),
                  pl.BlockSpec((1, D_MODEL), lambda i: (0, 0))],
        out_specs=pl.BlockSpec((tm, D_MODEL), lambda i: (i, 0)),
        out_shape=jax.ShapeDtypeStruct((m, D_MODEL), F32),
        compiler_params=_params("arbitrary"),
        name="final_norm",
    )(x2, final_g2)


def _final_norm_seqs_body(*refs):
    *x_refs, g_ref, o_ref = refs
    b = pl.program_id(0)
    x = x_refs[-1][...]
    for k in range(len(x_refs) - 2, -1, -1):
        x = jnp.where(b == k, x_refs[k][...], x)
    o_ref[...] = x * lax.rsqrt(jnp.mean(x * x, axis=-1, keepdims=True) + EPS) * g_ref[...]


def _final_norm_seqs(xs, final_g2, tm):
    seq = xs[0].shape[0]
    last = seq // tm - 1

    def xspec(k):
        return pl.BlockSpec((tm, D_MODEL), lambda b, i: (jnp.where(b == k, i, jnp.where(b > k, last, 0)), 0))

    return pl.pallas_call(
        _final_norm_seqs_body,
        grid=(len(xs), seq // tm),
        in_specs=[xspec(k) for k in range(len(xs))] + [pl.BlockSpec((1, D_MODEL), lambda b, i: (0, 0))],
        out_specs=pl.BlockSpec((None, tm, D_MODEL), lambda b, i: (b, i, 0)),
        out_shape=jax.ShapeDtypeStruct((len(xs), seq, D_MODEL), F32),
        compiler_params=_params("arbitrary", "arbitrary"),
        name="final_norm_seqs",
    )(*xs, final_g2)


def _rope_tables(pos):
    half = HEAD_DIM // 2
    inv = ROPE_THETA ** (-jnp.arange(half, dtype=F32) / half)
    ang = pos.astype(F32)[:, None] * inv[None, :]
    cos, sin = jnp.cos(ang), jnp.sin(ang)
    return jnp.concatenate([cos, cos], axis=-1), jnp.concatenate([-sin, sin], axis=-1)


def kernel(x_prompt, x_sample, c_prompt, c_sample, cache_win_k, cache_win_v, state_ssm_re, state_ssm_im, w_ada, b_ada, norm_g, w_in, ssm_a_re, ssm_a_im, ssm_log_dt, ssm_b_re, ssm_b_im, ssm_c_re, ssm_c_im, ssm_d, w_glu, b_glu, attn_sinks, w_out, final_g):
    bp, seq, _ = x_prompt.shape
    bs = x_sample.shape[0]
    past_len = 8192
    assert seq % (CHUNK * 128) == 0 and x_sample.shape[1] == 1 and cache_win_k.shape[2] == WINDOW
    assert bp + bs <= MOD_ROWS

    w_in_b = jnp.concatenate(
        [w_in[..., 0:2048], w_in[..., 4096:6144], w_in[..., 7168:9216], w_in[..., 2048:4096],
         w_in[..., 6144:6656], w_in[..., 6656:7168]], axis=-1).astype(BF16)
    w_glu_b = w_glu.astype(BF16)
    w_out_b4 = w_out.astype(BF16).reshape(DEPTH, 2, D_SSM, D_MODEL)
    norm_g3 = norm_g.reshape(DEPTH, 1, D_MODEL)
    b_glu3 = b_glu.reshape(DEPTH, 1, D_SSM)
    ssm_d3 = ssm_d.reshape(DEPTH, 1, D_SSM)

    c_all = jnp.concatenate([c_prompt, c_sample, jnp.zeros((MOD_ROWS - bp - bs, D_MODEL), F32)], axis=0)
    mod = _modulation(c_all, w_ada, b_ada)
    mod_p = mod[:, :bp].reshape(DEPTH, bp, 1, 3 * D_MODEL)
    mod_s = mod[:, bp:bp + bs].reshape(DEPTH, 1, bs, 3 * D_MODEL)

    prep = _ssm_prep(ssm_a_re, ssm_a_im, ssm_log_dt, ssm_b_re, ssm_b_im, ssm_c_re, ssm_c_im)
    cos_p, sin_p = _rope_tables(jnp.arange(seq))
    cos_s, sin_s = _rope_tables(jnp.full((bs,), past_len))

    xcur = [(x_prompt, k) for k in range(bp)]
    pk = [[None] * bp for _ in range(DEPTH)]
    pv = [[None] * bp for _ in range(DEPTH)]
    pr = [[None] * bp for _ in range(DEPTH)]
    pi = [[None] * bp for _ in range(DEPTH)]

    def finish(z, l, k, sa):
        yg, hr, hi, oa = sa
        ys = _glu(yg, z, Z_GS, w_glu_b, b_glu3, l, 512)
        xcur[k] = (_out_proj(ys, oa, w_out_b4, xcur[k][0], xcur[k][1], mod_p, k, l, 512)[None], 0)
        pk[l][k] = z[seq - WINDOW:, Z_K:Z_K + D_KV].astype(F32).reshape(WINDOW, N_KV_HEADS, HEAD_DIM)
        pv[l][k] = z[seq - WINDOW:, Z_V:Z_V + D_KV].astype(F32).reshape(WINDOW, N_KV_HEADS, HEAD_DIM)
        pr[l][k] = hr.reshape(N_GROUPS, SSM_STATE)
        pi[l][k] = hi.reshape(N_GROUPS, SSM_STATE)

    pending = None
    for l in range(DEPTH):
        for k in range(bp):
            h = _rms_mod(xcur[k][0], xcur[k][1], norm_g3, mod_p, k, l, 256)
            if pending is None:
                z, = _prompt_stage(h, l, w_in_b, cos_p, sin_p, None, None, prep, ssm_d3, attn_sinks)
            else:
                z, *sa = _prompt_stage(h, l, w_in_b, cos_p, sin_p, pending[0], pending[1], prep, ssm_d3,
                                       attn_sinks)
                finish(*pending, sa)
            pending = (z, l, k)
    sa = _prompt_stage(None, None, w_in_b, cos_p, sin_p, pending[0], pending[1], prep, ssm_d3, attn_sinks)
    finish(*pending, sa)
    y_prompt = _final_norm_seqs([x[0] for x, _ in xcur], final_g.reshape(1, D_MODEL), 256)
    pk, pv, pr, pi = (jnp.stack([jnp.stack(row) for row in t]) for t in (pk, pv, pr, pi))

    xs = x_sample.reshape(bs, D_MODEL)
    sk, sv, sr, si = [], [], [], []
    for l in range(DEPTH):
        h = _rms_mod(xs[None], 0, norm_g3, mod_s, 0, l, bs)
        u, zb = _in_proj(h, w_in_b, cos_s, sin_s, l, bs)
        ut = u.reshape(bs, N_GROUPS, SSM_GROUP).transpose(1, 2, 0)
        ht_r = state_ssm_re[l].transpose(1, 2, 0)
        ht_i = state_ssm_im[l].transpose(1, 2, 0)
        ygt, nhr, nhi = _ssm_step(ut, ht_r, ht_i, prep, ssm_b_re, ssm_b_im, ssm_c_re, ssm_c_im, ssm_d, l)
        yg = ygt.transpose(2, 0, 1).reshape(bs, D_SSM).astype(BF16)
        zf = zb.astype(F32)
        q3 = zf[:, ZB_Q:ZB_Q + D_ATT].reshape(bs, N_HEADS, HEAD_DIM)
        ga3 = zf[:, ZB_GA:ZB_GA + D_ATT].reshape(bs, N_HEADS, HEAD_DIM)
        kn = zf[:, ZB_K:ZB_K + D_KV].reshape(bs, N_KV_HEADS, HEAD_DIM)
        vn = zf[:, ZB_V:ZB_V + D_KV].reshape(bs, N_KV_HEADS, HEAD_DIM)
        oa = _attention_step(q3, ga3, kn, vn, cache_win_k, cache_win_v, attn_sinks, l)
        ys = _glu(yg, zb, ZB_GS, w_glu_b, b_glu3, l, bs)
        xs = _out_proj(ys, oa.reshape(bs, D_ATT).astype(BF16), w_out_b4, xs[None], 0, mod_s, 0, l, bs)
        sk.append(kn.reshape(bs, 1, N_KV_HEADS, HEAD_DIM))
        sv.append(vn.reshape(bs, 1, N_KV_HEADS, HEAD_DIM))
        sr.append(nhr.transpose(2, 0, 1))
        si.append(nhi.transpose(2, 0, 1))
    y_sample = _final_norm(xs, final_g.reshape(1, D_MODEL), bs).reshape(bs, 1, D_MODEL)
    new_k, new_v = _roll_cache(cache_win_k, cache_win_v, jnp.stack(sk), jnp.stack(sv))

    return (y_prompt, y_sample, jnp.stack(pk), jnp.stack(pv), jnp.stack(pr), jnp.stack(pi),
            new_k, new_v, jnp.stack(sr), jnp.stack(si))
```

```python
import functools
import math

import jax
import jax.numpy as jnp
from jax import lax
from jax.experimental import pallas as pl
from jax.experimental.pallas import tpu as pltpu

F32 = jnp.float32
BF16 = jnp.bfloat16

D_MODEL = 4096
DEPTH = 4
D_SSM = 2048
D_ATT = 2048
SSM_GROUP = 16
N_GROUPS = 128
SSM_STATE = 64
HEAD_DIM = 128
N_HEADS = 16
N_KV_HEADS = 4
KV_GROUP = 4
D_KV = 512
WINDOW = 128
ROPE_THETA = 10000.0
EPS = 1e-6
BLOCK = 128
D_IN = 9216
ATT_SCALE = HEAD_DIM ** -0.5

IN_TN = 512
IN_TILES = D_IN // IN_TN
U_TILES = D_SSM // IN_TN
ZB_COLS = D_IN - D_SSM
ZB_Q, ZB_GA, ZB_GS, ZB_K, ZB_V = 0, 2048, 4096, 6144, 6656

CHUNK = 16
SSM_GB = 8
SSM_W = SSM_GB * SSM_GROUP
SCAN_LANES = 128
MOD_ROWS = 40

VMEM_LIMIT_BYTES = 56 * 1024 * 1024


def _params(*sem):
    return pltpu.CompilerParams(dimension_semantics=sem, vmem_limit_bytes=VMEM_LIMIT_BYTES)


def _mod_body(c_ref, w_ref, b_ref, o_ref):
    a = jax.nn.silu(c_ref[...]).astype(BF16)
    o_ref[...] = jnp.dot(a, w_ref[...].astype(BF16), preferred_element_type=F32) + b_ref[...]


def _modulation(c_all, w_ada, b_ada):
    tn = 512
    return pl.pallas_call(
        _mod_body,
        grid=(DEPTH, 3 * D_MODEL // tn),
        in_specs=[pl.BlockSpec((MOD_ROWS, D_MODEL), lambda l, j: (0, 0)),
                  pl.BlockSpec((None, D_MODEL, tn), lambda l, j: (l, 0, j)),
                  pl.BlockSpec((None, 1, tn), lambda l, j: (l, 0, j))],
        out_specs=pl.BlockSpec((None, MOD_ROWS, tn), lambda l, j: (l, 0, j)),
        out_shape=jax.ShapeDtypeStruct((DEPTH, MOD_ROWS, 3 * D_MODEL), F32),
        compiler_params=_params("arbitrary", "arbitrary"),
        name="adaln_mod",
    )(c_all, w_ada, b_ada.reshape(DEPTH, 1, 3 * D_MODEL))


def _rmsmod_body(x_ref, g_ref, sc_ref, sh_ref, o_ref):
    x = x_ref[...]
    y = x * lax.rsqrt(jnp.mean(x * x, axis=-1, keepdims=True) + EPS)
    o_ref[...] = ((y * g_ref[...]) * (1.0 + sc_ref[...]) + sh_ref[...]).astype(BF16)


def _rms_mod(x3, kx, norm_g3, mod4, kb, l, tm):
    lr = x3.shape[1]
    r = mod4.shape[2]
    return pl.pallas_call(
        _rmsmod_body,
        grid=(lr // tm,),
        in_specs=[pl.BlockSpec((None, tm, D_MODEL), lambda i: (kx, i, 0)),
                  pl.BlockSpec((None, 1, D_MODEL), lambda i: (l, 0, 0)),
                  pl.BlockSpec((None, None, r, D_MODEL), lambda i: (l, kb, 0, 1)),
                  pl.BlockSpec((None, None, r, D_MODEL), lambda i: (l, kb, 0, 0))],
        out_specs=pl.BlockSpec((tm, D_MODEL), lambda i: (i, 0)),
        out_shape=jax.ShapeDtypeStruct((lr, D_MODEL), BF16),
        compiler_params=_params("arbitrary"),
        name="rms_mod",
    )(x3, norm_g3, mod4, mod4)


def _w_in_tile(j):
    return jnp.where(j < 4, j, jnp.where(j < 8, j + 4, jnp.where(j < 12, j + 6, jnp.where(j < 16, j - 8, j - 4))))


def _rope(x, cos, sin):
    return x * cos + pltpu.roll(x, HEAD_DIM // 2, 1) * sin


def _inproj_body(h_ref, w_ref, cos_ref, sin_ref, u_ref, zb_ref):
    j = pl.program_id(1)
    half = IN_TN // 2

    def halves(epilogue):
        for c in range(2):
            sl = slice(c * half, (c + 1) * half)
            epilogue(sl, jnp.dot(h_ref[...], w_ref[:, sl], preferred_element_type=F32))

    @pl.when(j < 4)
    def _():
        def plain(sl, acc):
            u_ref[:, sl] = acc
        halves(plain)

    @pl.when(((j >= 4) & (j < 8)) | (j == 16))
    def _():
        def rope(sl, acc):
            for hh in range(half // HEAD_DIM):
                hs = slice(hh * HEAD_DIM, (hh + 1) * HEAD_DIM)
                zb_ref[:, sl.start + hs.start:sl.start + hs.stop] = _rope(
                    acc[:, hs], cos_ref[...], sin_ref[...]).astype(BF16)
        halves(rope)

    @pl.when((j >= 8) & (j < 16))
    def _():
        def silu(sl, acc):
            zb_ref[:, sl] = jax.nn.silu(acc).astype(BF16)
        halves(silu)

    @pl.when(j == 17)
    def _():
        def cast(sl, acc):
            zb_ref[:, sl] = acc.astype(BF16)
        halves(cast)


def _in_proj(h2, w_in_b, cos_t, sin_t, l, tm):
    m = h2.shape[0]
    nt = cos_t.shape[0] // tm
    return pl.pallas_call(
        _inproj_body,
        grid=(m // tm, IN_TILES),
        in_specs=[pl.BlockSpec((tm, D_MODEL), lambda i, j: (i, 0)),
                  pl.BlockSpec((None, D_MODEL, IN_TN), lambda i, j: (l, 0, _w_in_tile(j))),
                  pl.BlockSpec((tm, HEAD_DIM), lambda i, j: (i % nt, 0)),
                  pl.BlockSpec((tm, HEAD_DIM), lambda i, j: (i % nt, 0))],
        out_specs=[pl.BlockSpec((tm, IN_TN), lambda i, j: (i, jnp.minimum(j, U_TILES - 1))),
                   pl.BlockSpec((tm, IN_TN), lambda i, j: (i, jnp.maximum(j - U_TILES, 0)))],
        out_shape=[jax.ShapeDtypeStruct((m, D_SSM), F32),
                   jax.ShapeDtypeStruct((m, ZB_COLS), BF16)],
        compiler_params=_params("arbitrary", "arbitrary"),
        name="in_proj",
    )(h2, w_in_b, cos_t, sin_t)


def _cmul(ar, ai, br, bi):
    return ar * br - ai * bi, ar * bi + ai * br


def _discretise(ar, ai, log_dt):
    dt = jnp.exp(log_dt)
    mag = jnp.exp(ar * dt)
    lr = mag * jnp.cos(ai * dt)
    li = mag * jnp.sin(ai * dt)
    nr = lr - 1.0
    ni = li
    den = ar * ar + ai * ai
    qr = (nr * ar + ni * ai) / den
    qi = (ni * ar - nr * ai) / den
    return lr, li, qr, qi


def _powers(lr, li, n):
    pr, pi = [jnp.ones_like(lr)], [jnp.zeros_like(li)]
    for _ in range(n):
        r, i = _cmul(pr[-1], pi[-1], lr, li)
        pr.append(r)
        pi.append(i)
    return pr, pi


PREP_GB = 16


def _split_bf16(x):
    hi = x.astype(BF16)
    return hi, (x - hi.astype(F32)).astype(BF16)


def _ssm_prep_body(a2r_ref, a2i_ref, ldt_ref, cc_ref, cs_ref, bc_ref, bs_ref,
                   mt_ref, ws_ref, wc_ref, l16r_ref, l16i_ref, l1r_ref, l1i_ref, qr_ref, qi_ref):
    gb, t_len, p = PREP_GB, CHUNK, SSM_STATE
    kdim = t_len * SSM_GROUP
    lr, li, qr, qi = _discretise(a2r_ref[...], a2i_ref[...], ldt_ref[...])
    sgn = jnp.where(lax.broadcasted_iota(jnp.int32, (1, 2 * p), 1) < p, 1.0, -1.0)
    pr, pi = _powers(lr, li, t_len)
    mag2 = lr * lr + li * li
    ir, ii = _powers(lr / mag2, -li / mag2, t_len - 1)
    prs = [x * sgn for x in pr]
    w_inv = [_cmul(qr, qi, ir[s], ii[s]) for s in range(t_len)]
    w_15 = [_cmul(qr, qi, pr[t_len - 1 - s], pi[t_len - 1 - s]) for s in range(t_len)]
    w_inv = [(wr, wi * sgn) for wr, wi in w_inv]
    w_15 = [(wr, wi * sgn) for wr, wi in w_15]

    row_t = lax.broadcasted_iota(jnp.int32, (kdim, kdim), 0) >> 4
    col_s = lax.broadcasted_iota(jnp.int32, (kdim, kdim), 1) >> 4
    causal = row_t >= col_s
    nt = (((1,), (1,)), ((), ()))
    for g in range(gb):
        row = slice(g, g + 1)
        cc, cs = cc_ref[g], cs_ref[g]
        bc, bs = bc_ref[g], bs_ref[g]
        cl = jnp.concatenate([cc * prs[t][row] - cs * pi[t][row] for t in range(t_len)], axis=0)
        wc = jnp.concatenate([cc * prs[t + 1][row] - cs * pi[t + 1][row] for t in range(t_len)], axis=0)
        bl = jnp.concatenate([bc * w_inv[s][0][row] - bs * w_inv[s][1][row] for s in range(t_len)], axis=0)
        ws = jnp.concatenate([bc * w_15[s][0][row] - bs * w_15[s][1][row] for s in range(t_len)], axis=0)
        a_hi, a_lo = _split_bf16(cl)
        b_hi, b_lo = _split_bf16(bl)
        m = (lax.dot_general(a_hi, b_hi, nt, preferred_element_type=F32)
             + lax.dot_general(a_hi, b_lo, nt, preferred_element_type=F32)
             + lax.dot_general(a_lo, b_hi, nt, preferred_element_type=F32))
        mt_ref[g] = jnp.where(causal, m, 0.0).astype(BF16)
        ws_ref[g] = ws.T.astype(BF16)
        wc_ref[g] = wc.astype(BF16)

    pad = jnp.zeros((2 * p - gb, 2 * p), F32)
    outs = ((pr[t_len], l16r_ref), (pi[t_len], l16i_ref), (lr, l1r_ref), (li, l1i_ref),
            (qr, qr_ref), (qi, qi_ref))
    for val, ref in outs:
        col = jnp.concatenate([val, pad], axis=0).T
        for g in range(gb):
            ref[g] = jnp.broadcast_to(col[:p, g:g + 1], ref.shape[1:])


def _ssm_prep(a_re, a_im, log_dt, b_re, b_im, c_re, c_im):
    g, p, kdim, gb = N_GROUPS, SSM_STATE, CHUNK * SSM_GROUP, PREP_GB
    a2r = jnp.concatenate([a_re, a_re], axis=-1)
    a2i = jnp.concatenate([a_im, a_im], axis=-1)
    ldt = log_dt.reshape(DEPTH, g, 1)
    bt_re = jnp.swapaxes(b_re, -1, -2)
    bt_im = jnp.swapaxes(b_im, -1, -2)
    cc = jnp.concatenate([c_re, c_im], axis=-1)
    cs = jnp.concatenate([c_im, c_re], axis=-1)
    bc = jnp.concatenate([bt_re, bt_im], axis=-1)
    bs = jnp.concatenate([bt_im, bt_re], axis=-1)

    def spec(a, b):
        return pl.BlockSpec((None, gb, a, b), lambda l, i: (l, i, 0, 0))

    def spec2(b):
        return pl.BlockSpec((None, gb, b), lambda l, i: (l, i, 0))

    col_shape = jax.ShapeDtypeStruct((DEPTH, g, p, 1), F32)
    wide_shape = jax.ShapeDtypeStruct((DEPTH, g, p, SCAN_LANES), F32)
    return pl.pallas_call(
        _ssm_prep_body,
        grid=(DEPTH, g // gb),
        in_specs=[spec2(2 * p), spec2(2 * p), spec2(1),
                  spec(SSM_GROUP, 2 * p), spec(SSM_GROUP, 2 * p), spec(SSM_GROUP, 2 * p),
                  spec(SSM_GROUP, 2 * p)],
        out_specs=([spec(kdim, kdim), spec(2 * p, kdim), spec(kdim, 2 * p)]
                   + [spec(p, SCAN_LANES)] * 2 + [spec(p, 1)] * 4),
        out_shape=[jax.ShapeDtypeStruct((DEPTH, g, kdim, kdim), BF16),
                   jax.ShapeDtypeStruct((DEPTH, g, 2 * p, kdim), BF16),
                   jax.ShapeDtypeStruct((DEPTH, g, kdim, 2 * p), BF16)] + [wide_shape] * 2 + [col_shape] * 4,
        compiler_params=_params("arbitrary", "arbitrary"),
        name="ssm_prep",
    )(a2r, a2i, ldt, cc, cs, bc, bs)


def _ssm_part(u_ref, mt_ref, ws_ref, wc_ref, l16r_ref, l16i_ref, d_ref,
              yg_ref, hr_ref, hi_ref, ubuf, ybuf):
    gb, t_len, p = SSM_GB, CHUNK, SSM_STATE
    r_len = u_ref.shape[0] // t_len
    ubuf[...] = u_ref[...].astype(F32)
    slab_t = [ubuf[pl.ds(s, r_len, stride=t_len), :].T.astype(BF16) for s in range(t_len)]

    yl, xr, xi = [], [], []
    for g in range(gb):
        rows = slice(g * SSM_GROUP, (g + 1) * SSM_GROUP)
        s_g = jnp.concatenate([st[rows, :] for st in slab_t], axis=0)
        both = jnp.dot(jnp.concatenate([mt_ref[g], ws_ref[g]], axis=0), s_g, preferred_element_type=F32)
        kd = t_len * SSM_GROUP
        yl.append(both[:kd, :])
        xr.append(both[kd:kd + p, :])
        xi.append(both[kd + p:, :])
    h_r = jnp.concatenate(xr, axis=0)
    h_i = jnp.concatenate(xi, axis=0)
    yield

    a_r = l16r_ref[...].reshape(gb * p, r_len)
    a_i = l16i_ref[...].reshape(gb * p, r_len)
    lane = lax.broadcasted_iota(jnp.int32, (gb * p, r_len), 1)
    k = 1
    while k < r_len:
        keep = lane >= k
        s_r = jnp.where(keep, pltpu.roll(h_r, k, 1), 0.0)
        s_i = jnp.where(keep, pltpu.roll(h_i, k, 1), 0.0)
        h_r, h_i = h_r + a_r * s_r - a_i * s_i, h_i + a_r * s_i + a_i * s_r
        a_r, a_i = a_r * a_r - a_i * a_i, 2.0 * a_r * a_i
        k *= 2
    hr_ref[...] = h_r[:, r_len - 1:r_len].reshape(gb, p, 1)
    hi_ref[...] = h_i[:, r_len - 1:r_len].reshape(gb, p, 1)

    first = lane >= 1
    c_r = jnp.where(first, pltpu.roll(h_r, 1, 1), 0.0).astype(BF16)
    c_i = jnp.where(first, pltpu.roll(h_i, 1, 1), 0.0).astype(BF16)
    ys = []
    for g in range(gb):
        rows = slice(g * p, (g + 1) * p)
        c_g = jnp.concatenate([c_r[rows, :], c_i[rows, :]], axis=0)
        ys.append(yl[g] + jnp.dot(wc_ref[g], c_g, preferred_element_type=F32))
    yield
    d = d_ref[...]
    for t in range(t_len):
        rows = slice(t * SSM_GROUP, (t + 1) * SSM_GROUP)
        y_t = jnp.concatenate([y[rows, :] for y in ys], axis=0)
        u_t = ubuf[pl.ds(t, r_len, stride=t_len), :]
        ybuf[pl.ds(t, r_len, stride=t_len), :] = jax.nn.gelu(y_t.T + d * u_t)
    yg_ref[...] = ybuf[...].astype(BF16)


SSM_STEP_GB = 16


def _ssm_step_body(lr_ref, li_ref, qr_ref, qi_ref, br_ref, bi_ref, cr_ref, ci_ref, d_ref, ut_ref,
                   hr_ref, hi_ref, yt_ref, nhr_ref, nhi_ref):
    lr, li = lr_ref[...], li_ref[...]
    bbr, bbi = _cmul(qr_ref[...], qi_ref[...], br_ref[...], bi_ref[...])
    for g in range(SSM_STEP_GB):
        u = ut_ref[g]
        ub = u.astype(BF16)
        x_r = jnp.dot(bbr[g].astype(BF16), ub, preferred_element_type=F32)
        x_i = jnp.dot(bbi[g].astype(BF16), ub, preferred_element_type=F32)
        h_r, h_i = hr_ref[g], hi_ref[g]
        n_r = lr[g] * h_r - li[g] * h_i + x_r
        n_i = lr[g] * h_i + li[g] * h_r + x_i
        y = (jnp.dot(cr_ref[g].astype(BF16), n_r.astype(BF16), preferred_element_type=F32)
             - jnp.dot(ci_ref[g].astype(BF16), n_i.astype(BF16), preferred_element_type=F32))
        yt_ref[g] = jax.nn.gelu(y + d_ref[g] * u)
        nhr_ref[g] = n_r
        nhi_ref[g] = n_i


def _ssm_step(ut, ht_r, ht_i, prep, b_re, b_im, c_re, c_im, ssm_d, l):
    l1r, l1i, qr, qi = prep[5:9]
    g, p, gb = N_GROUPS, SSM_STATE, SSM_STEP_GB
    bs = ut.shape[-1]

    def lspec(a, c):
        return pl.BlockSpec((None, gb, a, c), lambda i: (l, i, 0, 0))

    def aspec(a):
        return pl.BlockSpec((gb, a, bs), lambda i: (i, 0, 0))

    return pl.pallas_call(
        _ssm_step_body,
        grid=(g // gb,),
        in_specs=[lspec(p, 1), lspec(p, 1), lspec(p, 1), lspec(p, 1), lspec(p, SSM_GROUP), lspec(p, SSM_GROUP),
                  lspec(SSM_GROUP, p), lspec(SSM_GROUP, p), lspec(SSM_GROUP, 1),
                  aspec(SSM_GROUP), aspec(p), aspec(p)],
        out_specs=[aspec(SSM_GROUP), aspec(p), aspec(p)],
        out_shape=[jax.ShapeDtypeStruct((g, SSM_GROUP, bs), F32),
                   jax.ShapeDtypeStruct((g, p, bs), F32),
                   jax.ShapeDtypeStruct((g, p, bs), F32)],
        compiler_params=_params("arbitrary"),
        name="ssm_step",
    )(l1r, l1i, qr, qi, b_re, b_im, c_re, c_im, ssm_d.reshape(DEPTH, g, SSM_GROUP, 1), ut, ht_r, ht_i)


def _attn_part(l, n, q_ref, ga_ref, kc_ref, kp_ref, vc_ref, vp_ref, sink_ref, o_ref):
    cols = KV_GROUP * BLOCK
    j = lax.broadcasted_iota(jnp.int32, (2 * BLOCK, cols), 0)
    r = lax.broadcasted_iota(jnp.int32, (2 * BLOCK, cols), 1) & (BLOCK - 1)
    no_prev = jnp.where(n > 0, 0, 2 * BLOCK)
    valid = ((j < BLOCK) & (j > r + no_prev)) | ((j >= BLOCK) & (j - BLOCK <= r))
    ones = jnp.ones((HEAD_DIM, 2 * BLOCK), BF16)
    nt = (((1,), (1,)), ((), ()))
    heads = [[kh * KV_GROUP + g for g in range(KV_GROUP)] for kh in range(N_KV_HEADS)]
    scores = []
    for kh in range(N_KV_HEADS):
        ksl = slice(kh * HEAD_DIM, (kh + 1) * HEAD_DIM)
        k = jnp.concatenate([kp_ref[:, ksl], kc_ref[:, ksl]], axis=0)
        q4 = jnp.concatenate([q_ref[:, h * HEAD_DIM:(h + 1) * HEAD_DIM] for h in heads[kh]], axis=0)
        scores.append(lax.dot_general(k, q4, nt, preferred_element_type=F32))
    yield
    outs = []
    for kh in range(N_KV_HEADS):
        ksl = slice(kh * HEAD_DIM, (kh + 1) * HEAD_DIM)
        vt = jnp.concatenate([vp_ref[:, ksl].T, vc_ref[:, ksl].T], axis=1)
        v1t = jnp.concatenate([vt, ones], axis=0)
        st = jnp.where(valid, scores[kh] * ATT_SCALE, -jnp.inf)
        sink = jnp.concatenate([jnp.full((1, BLOCK), sink_ref[l, h], F32) for h in heads[kh]], axis=1)
        m = jnp.maximum(jnp.max(st, axis=0, keepdims=True), sink)
        e = jnp.exp(st - m).astype(BF16)
        o1 = jnp.dot(v1t, e, preferred_element_type=F32)
        outs.append((o1, jnp.exp(sink - m)))
    yield
    for kh in range(N_KV_HEADS):
        o1, e_sink = outs[kh]
        ot = o1[:HEAD_DIM, :] / (o1[HEAD_DIM:, :] + e_sink)
        for g, h in enumerate(heads[kh]):
            hsl = slice(h * HEAD_DIM, (h + 1) * HEAD_DIM)
            o_ref[:, hsl] = (ot[:, g * BLOCK:(g + 1) * BLOCK].T * ga_ref[:, hsl].astype(F32)).astype(BF16)


Z_Q, Z_GA, Z_GS, Z_K, Z_V = 2048, 4096, 6144, 8192, 8704
SA_STEPS = N_GROUPS // SSM_GB
IN_ROWS = 512


def _in_part(j, h_ref, w_ref, cos_ref, sin_ref, z_ref):
    is_rope = ((j >= 4) & (j < 8)) | (j == 16)
    is_silu = (j >= 8) & (j < 16)
    for rc in range(h_ref.shape[0] // IN_ROWS):
        rs = slice(rc * IN_ROWS, (rc + 1) * IN_ROWS)
        acc = jnp.dot(h_ref[rs, :], w_ref[...], preferred_element_type=F32)
        cos, sin = cos_ref[rs, :], sin_ref[rs, :]
        roped = jnp.concatenate([_rope(acc[:, hh * HEAD_DIM:(hh + 1) * HEAD_DIM], cos, sin)
                                 for hh in range(IN_TN // HEAD_DIM)], axis=1)
        out = jnp.where(is_rope, roped, jnp.where(is_silu, jax.nn.silu(acc), acc))
        z_ref[rs, :] = out.astype(BF16)
        yield


def _stage_body(has_in, has_sa, l_prev, *refs):
    j = pl.program_id(0)
    refs = list(refs)
    in_refs = [refs.pop(0) for _ in range(4)] if has_in else []
    sa_refs = [refs.pop(0) for _ in range(14)] if has_sa else []
    proj = _in_part(j, *in_refs, refs.pop(0)) if has_in else iter(())
    side = []
    if has_sa:
        u_ref, q_ref, ga_ref, kc_ref, kp_ref, vc_ref, vp_ref = sa_refs[:7]
        mt_ref, ws_ref, wc_ref, l16r_ref, l16i_ref, d_ref, sink_ref = sa_refs[7:]
        yg_ref, hr_ref, hi_ref, oa_ref, ubuf, ybuf = refs
        side = [_ssm_part(u_ref, mt_ref, ws_ref, wc_ref, l16r_ref, l16i_ref, d_ref, yg_ref, hr_ref,
                          hi_ref, ubuf, ybuf),
                _attn_part(l_prev, jnp.minimum(j, SA_STEPS - 1), q_ref, ga_ref, kc_ref, kp_ref, vc_ref,
                           vp_ref, sink_ref, oa_ref)]
    live = True
    while live:
        live = False
        for gen in side + [proj]:
            if next(gen, "done") != "done":
                live = True


def _prompt_stage(h, l_in, w_in_b, cos_t, sin_t, z_prev, l_prev, prep, ssm_d3, sinks):
    has_in, has_sa = h is not None, z_prev is not None
    seq = (h if has_in else z_prev).shape[0]
    gb, p, kdim = SSM_GB, SSM_STATE, CHUNK * SSM_GROUP
    in_specs, args, out_specs, out_shape, scratch = [], [], [], [], []

    def jm(j):
        return jnp.minimum(j, SA_STEPS - 1)

    if has_in:
        table = pl.BlockSpec((seq, HEAD_DIM), lambda j: (0, 0))
        in_specs += [pl.BlockSpec((seq, D_MODEL), lambda j: (0, 0), pipeline_mode=pl.Buffered(1)),
                     pl.BlockSpec((None, D_MODEL, IN_TN), lambda j: (l_in, 0, _w_in_tile(j))), table, table]
        args += [h, w_in_b, cos_t, sin_t]
        out_specs.append(pl.BlockSpec((seq, IN_TN), lambda j: (0, j)))
        out_shape.append(jax.ShapeDtypeStruct((seq, D_IN), BF16))
    if has_sa:
        def wspec(a, c):
            return pl.BlockSpec((None, gb, a, c), lambda j: (l_prev, jm(j), 0, 0))

        def kv(col, prev):
            if prev:
                return pl.BlockSpec((BLOCK, D_KV), lambda j: (jnp.maximum(jm(j) - 1, 0), col // D_KV))
            return pl.BlockSpec((BLOCK, D_KV), lambda j: (jm(j), col // D_KV))

        st_spec = pl.BlockSpec((gb, p, 1), lambda j: (jm(j), 0, 0))
        in_specs += [pl.BlockSpec((seq, SSM_W), lambda j: (0, jm(j))),
                     pl.BlockSpec((BLOCK, D_ATT), lambda j: (jm(j), Z_Q // D_ATT)),
                     pl.BlockSpec((BLOCK, D_ATT), lambda j: (jm(j), Z_GA // D_ATT)),
                     kv(Z_K, False), kv(Z_K, True), kv(Z_V, False), kv(Z_V, True),
                     wspec(kdim, kdim), wspec(2 * p, kdim), wspec(kdim, 2 * p),
                     wspec(p, SCAN_LANES), wspec(p, SCAN_LANES),
                     pl.BlockSpec((None, 1, SSM_W), lambda j: (l_prev, 0, jm(j))),
                     pl.BlockSpec(memory_space=pltpu.SMEM)]
        args += [z_prev] * 7 + list(prep[:5]) + [ssm_d3, sinks]
        out_specs += [pl.BlockSpec((seq, SSM_W), lambda j: (0, jm(j))), st_spec, st_spec,
                      pl.BlockSpec((BLOCK, D_ATT), lambda j: (jm(j), 0))]
        out_shape += [jax.ShapeDtypeStruct((seq, D_SSM), BF16),
                      jax.ShapeDtypeStruct((N_GROUPS, p, 1), F32),
                      jax.ShapeDtypeStruct((N_GROUPS, p, 1), F32),
                      jax.ShapeDtypeStruct((seq, D_ATT), BF16)]
        scratch = [pltpu.VMEM((seq, SSM_W), F32), pltpu.VMEM((seq, SSM_W), F32)]
    return pl.pallas_call(
        functools.partial(_stage_body, has_in, has_sa, l_prev),
        grid=(IN_TILES if has_in else SA_STEPS,),
        in_specs=in_specs, out_specs=out_specs, out_shape=out_shape, scratch_shapes=scratch,
        compiler_params=_params("arbitrary"),
        name="stage" + ("_in" if has_in else "") + ("_sa" if has_sa else ""),
    )(*args)


STEP_SB = 8


def _attn_step_body(l, q_ref, ga_ref, kn_ref, vn_ref, ck_ref, cv_ref, sink_ref, o_ref):
    rows = WINDOW * N_KV_HEADS
    head = lax.broadcasted_iota(jnp.int32, (N_HEADS, rows), 0)
    col = lax.broadcasted_iota(jnp.int32, (N_HEADS, rows), 1)
    valid = ((col & (N_KV_HEADS - 1)) == (head >> 2)) & (col >= N_KV_HEADS)
    sink = jnp.concatenate([jnp.full((1, 1), sink_ref[l, h], F32) for h in range(N_HEADS)], axis=0)
    nt = (((1,), (1,)), ((), ()))

    def per_head(x4):
        return jnp.concatenate([jnp.broadcast_to(x4[kh:kh + 1, :], (KV_GROUP, HEAD_DIM))
                                for kh in range(N_KV_HEADS)], axis=0)

    for b in range(STEP_SB):
        q = q_ref[b].astype(BF16)
        kn = per_head(kn_ref[b]).astype(BF16).astype(F32)
        vn = per_head(vn_ref[b]).astype(BF16).astype(F32)
        s = lax.dot_general(q, ck_ref[b].astype(BF16), nt, preferred_element_type=F32) * ATT_SCALE
        s = jnp.where(valid, s, -jnp.inf)
        s_new = jnp.sum(q.astype(F32) * kn, axis=-1, keepdims=True) * ATT_SCALE
        m = jnp.maximum(jnp.maximum(jnp.max(s, axis=-1, keepdims=True), s_new), sink)
        e = jnp.exp(s - m)
        e_new = jnp.exp(s_new - m)
        den = jnp.sum(e, axis=-1, keepdims=True) + e_new + jnp.exp(sink - m)
        o = (jnp.dot((e / den).astype(BF16), cv_ref[b].astype(BF16), preferred_element_type=F32)
             + (e_new / den).astype(BF16).astype(F32) * vn)
        o_ref[b] = o * ga_ref[b]


def _attention_step(q3, ga3, kn3, vn3, cache_k, cache_v, sinks, l):
    bs = q3.shape[0]
    sb = STEP_SB
    rows = WINDOW * N_KV_HEADS
    hspec = pl.BlockSpec((sb, N_HEADS, HEAD_DIM), lambda i: (i, 0, 0))
    nspec = pl.BlockSpec((sb, N_KV_HEADS, HEAD_DIM), lambda i: (i, 0, 0))
    cspec = pl.BlockSpec((None, sb, rows, HEAD_DIM), lambda i: (l, i, 0, 0))
    ck = cache_k.reshape(DEPTH, bs, rows, HEAD_DIM)
    cv = cache_v.reshape(DEPTH, bs, rows, HEAD_DIM)
    return pl.pallas_call(
        functools.partial(_attn_step_body, l),
        grid=(bs // sb,),
        in_specs=[hspec, hspec, nspec, nspec, cspec, cspec, pl.BlockSpec(memory_space=pltpu.SMEM)],
        out_specs=hspec,
        out_shape=jax.ShapeDtypeStruct((bs, N_HEADS, HEAD_DIM), F32),
        compiler_params=_params("arbitrary"),
        name="swa_step",
    )(q3, ga3, kn3, vn3, ck, cv, sinks)


def _roll_cache_body(ck_ref, cv_ref, kn_ref, vn_ref, ok_ref, ov_ref):
    for c_ref, n_ref, o_ref in ((ck_ref, kn_ref, ok_ref), (cv_ref, vn_ref, ov_ref)):
        o_ref[:, :WINDOW - 1] = c_ref[:, 1:]
        o_ref[:, WINDOW - 1:] = n_ref[...]


def _roll_cache(cache_k, cache_v, k_new, v_new):
    bs = cache_k.shape[1]
    sb = STEP_SB
    cspec = pl.BlockSpec((None, sb, WINDOW, N_KV_HEADS, HEAD_DIM), lambda l, i: (l, i, 0, 0, 0))
    nspec = pl.BlockSpec((None, sb, 1, N_KV_HEADS, HEAD_DIM), lambda l, i: (l, i, 0, 0, 0))
    shape = jax.ShapeDtypeStruct(cache_k.shape, F32)
    return pl.pallas_call(
        _roll_cache_body,
        grid=(DEPTH, bs // sb),
        in_specs=[cspec, cspec, nspec, nspec],
        out_specs=[cspec, cspec],
        out_shape=[shape, shape],
        compiler_params=_params("arbitrary", "arbitrary"),
        name="roll_cache",
    )(cache_k, cache_v, k_new, v_new)


def _glu_body(y_ref, w_ref, b_ref, gs_ref, o_ref):
    y = y_ref[...]
    step = 512
    for c in range(D_SSM // step):
        sl = slice(c * step, (c + 1) * step)
        acc = jnp.dot(y, w_ref[:, sl], preferred_element_type=F32) + b_ref[:, sl]
        o_ref[:, sl] = (y_ref[:, sl].astype(F32) * jax.nn.sigmoid(acc)
                        * gs_ref[:, sl].astype(F32)).astype(BF16)


def _glu(yg2, zb2, gs_col, w_glu_b, b_glu3, l, tm):
    m = yg2.shape[0]
    return pl.pallas_call(
        _glu_body,
        grid=(m // tm,),
        in_specs=[pl.BlockSpec((tm, D_SSM), lambda i: (i, 0)),
                  pl.BlockSpec((None, D_SSM, D_SSM), lambda i: (l, 0, 0)),
                  pl.BlockSpec((None, 1, D_SSM), lambda i: (l, 0, 0)),
                  pl.BlockSpec((tm, D_SSM), lambda i: (i, gs_col // D_SSM))],
        out_specs=pl.BlockSpec((tm, D_SSM), lambda i: (i, 0)),
        out_shape=jax.ShapeDtypeStruct((m, D_SSM), BF16),
        compiler_params=_params("arbitrary"),
        name="glu",
    )(yg2, w_glu_b, b_glu3, zb2)


def _outproj_body(a_ref, b_ref, w1_ref, w2_ref, x_ref, g_ref, o_ref):
    acc = (jnp.dot(a_ref[...], w1_ref[...], preferred_element_type=F32)
           + jnp.dot(b_ref[...], w2_ref[...], preferred_element_type=F32))
    o_ref[...] = x_ref[...] + g_ref[...] * acc


def _out_proj(ys2, oa2, w_out_b4, x3, kx, mod4, kb, l, tm):
    m = x3.shape[1]
    tn = 1024
    r = mod4.shape[2]
    gate_blk = 2 * D_MODEL // tn
    return pl.pallas_call(
        _outproj_body,
        grid=(D_MODEL // tn, m // tm),
        in_specs=[pl.BlockSpec((tm, D_SSM), lambda j, i: (i, 0)),
                  pl.BlockSpec((tm, D_ATT), lambda j, i: (i, 0)),
                  pl.BlockSpec((None, None, D_SSM, tn), lambda j, i: (l, 0, 0, j)),
                  pl.BlockSpec((None, None, D_ATT, tn), lambda j, i: (l, 1, 0, j)),
                  pl.BlockSpec((None, tm, tn), lambda j, i: (kx, i, j)),
                  pl.BlockSpec((None, None, r, tn), lambda j, i: (l, kb, 0, gate_blk + j))],
        out_specs=pl.BlockSpec((tm, tn), lambda j, i: (i, j)),
        out_shape=jax.ShapeDtypeStruct((m, D_MODEL), F32),
        compiler_params=_params("arbitrary", "arbitrary"),
        name="out_proj",
    )(ys2, oa2, w_out_b4, w_out_b4, x3, mod4)


GO_TM = 256
GO_TN = 1024


def _mix_out_body(has_next, yg_ref, gs_ref, oa_ref, wg_ref, bg_ref, w1_ref, w2_ref, x_ref, gate_ref,
                  *rest):
    if has_next:
        ng_ref, sc_ref, sh_ref, xo_ref, h_ref, ys_buf, row_buf = rest
    else:
        xo_ref, ys_buf = rest
    j = pl.program_id(1)

    @pl.when(j == 0)
    def _():
        _glu_body(yg_ref, wg_ref, bg_ref, gs_ref, ys_buf)

    acc = (jnp.dot(ys_buf[...], w1_ref[...], preferred_element_type=F32)
           + jnp.dot(oa_ref[...], w2_ref[...], preferred_element_type=F32))
    xn = x_ref[...] + gate_ref[...] * acc
    xo_ref[...] = xn
    if has_next:
        row_buf[:, pl.ds(pl.multiple_of(j * GO_TN, GO_TN), GO_TN)] = xn

        @pl.when(j == D_MODEL // GO_TN - 1)
        def _():
            _rmsmod_body(row_buf, ng_ref, sc_ref, sh_ref, h_ref)


def _mix_out(yg, z, oa, w_glu_b, b_glu3, w_out_b4, x3, kx, mod4, kb, l, norm_g3):
    seq = yg.shape[0]
    tm, tn = GO_TM, GO_TN
    has_next = l + 1 < DEPTH
    gate_blk = 2 * D_MODEL // tn
    once = pl.Buffered(1)
    in_specs = [pl.BlockSpec((tm, D_SSM), lambda i, j: (i, 0)),
                pl.BlockSpec((tm, D_SSM), lambda i, j: (i, Z_GS // D_SSM)),
                pl.BlockSpec((tm, D_ATT), lambda i, j: (i, 0)),
                pl.BlockSpec((None, D_SSM, D_SSM), lambda i, j: (l, 0, 0), pipeline_mode=once),
                pl.BlockSpec((None, 1, D_SSM), lambda i, j: (l, 0, 0)),
                pl.BlockSpec((None, None, D_SSM, tn), lambda i, j: (l, 0, 0, j)),
                pl.BlockSpec((None, None, D_ATT, tn), lambda i, j: (l, 1, 0, j)),
                pl.BlockSpec((None, tm, tn), lambda i, j: (kx, i, j)),
                pl.BlockSpec((None, None, 1, tn), lambda i, j: (l, kb, 0, gate_blk + j))]
    args = [yg, z, oa, w_glu_b, b_glu3, w_out_b4, w_out_b4, x3, mod4]
    out_specs = [pl.BlockSpec((tm, tn), lambda i, j: (i, j))]
    out_shape = [jax.ShapeDtypeStruct((seq, D_MODEL), F32)]
    scratch = [pltpu.VMEM((tm, D_SSM), BF16)]
    if has_next:
        in_specs += [pl.BlockSpec((None, 1, D_MODEL), lambda i, j: (l + 1, 0, 0)),
                     pl.BlockSpec((None, None, 1, D_MODEL), lambda i, j: (l + 1, kb, 0, 1)),
                     pl.BlockSpec((None, None, 1, D_MODEL), lambda i, j: (l + 1, kb, 0, 0))]
        args += [norm_g3, mod4, mod4]
        out_specs.append(pl.BlockSpec((tm, D_MODEL), lambda i, j: (i, 0)))
        out_shape.append(jax.ShapeDtypeStruct((seq, D_MODEL), BF16))
        scratch.append(pltpu.VMEM((tm, D_MODEL), F32))
    return pl.pallas_call(
        functools.partial(_mix_out_body, has_next),
        grid=(seq // tm, D_MODEL // tn),
        in_specs=in_specs, out_specs=out_specs, out_shape=out_shape, scratch_shapes=scratch,
        compiler_params=_params("arbitrary", "arbitrary"),
        name="mix_out",
    )(*args)


def _final_norm_body(x_ref, g_ref, o_ref):
    x = x_ref[...]
    y = x * lax.rsqrt(jnp.mean(x * x, axis=-1, keepdims=True) + EPS)
    o_ref[...] = y * g_ref[...]


def _final_norm(x2, final_g2, tm):
    m = x2.shape[0]
    return pl.pallas_call(
        _final_norm_body,
        grid=(m // tm,),
        in_specs=[pl.BlockSpec((tm, D_MODEL), lambda i: (i, 0)),
                  pl.BlockSpec((1, D_MODEL), lambda i: (0, 0))],
        out_specs=pl.BlockSpec((tm, D_MODEL), lambda i: (i, 0)),
        out_shape=jax.ShapeDtypeStruct((m, D_MODEL), F32),
        compiler_params=_params("arbitrary"),
        name="final_norm",
    )(x2, final_g2)


def _final_norm_seqs_body(*refs):
    *x_refs, g_ref, o_ref = refs
    b = pl.program_id(0)
    x = x_refs[-1][...]
    for k in range(len(x_refs) - 2, -1, -1):
        x = jnp.where(b == k, x_refs[k][...], x)
    o_ref[...] = x * lax.rsqrt(jnp.mean(x * x, axis=-1, keepdims=True) + EPS) * g_ref[...]


def _final_norm_seqs(xs, final_g2, tm):
    seq = xs[0].shape[0]
    last = seq // tm - 1

    def xspec(k):
        return pl.BlockSpec((tm, D_MODEL), lambda b, i: (jnp.where(b == k, i, jnp.where(b > k, last, 0)), 0))

    return pl.pallas_call(
        _final_norm_seqs_body,
        grid=(len(xs), seq // tm),
        in_specs=[xspec(k) for k in range(len(xs))] + [pl.BlockSpec((1, D_MODEL), lambda b, i: (0, 0))],
        out_specs=pl.BlockSpec((None, tm, D_MODEL), lambda b, i: (b, i, 0)),
        out_shape=jax.ShapeDtypeStruct((len(xs), seq, D_MODEL), F32),
        compiler_params=_params("arbitrary", "arbitrary"),
        name="final_norm_seqs",
    )(*xs, final_g2)


def _rope_tables(pos):
    half = HEAD_DIM // 2
    inv = ROPE_THETA ** (-jnp.arange(half, dtype=F32) / half)
    ang = pos.astype(F32)[:, None] * inv[None, :]
    cos, sin = jnp.cos(ang), jnp.sin(ang)
    return jnp.concatenate([cos, cos], axis=-1), jnp.concatenate([-sin, sin], axis=-1)


def kernel(x_prompt, x_sample, c_prompt, c_sample, cache_win_k, cache_win_v, state_ssm_re, state_ssm_im, w_ada, b_ada, norm_g, w_in, ssm_a_re, ssm_a_im, ssm_log_dt, ssm_b_re, ssm_b_im, ssm_c_re, ssm_c_im, ssm_d, w_glu, b_glu, attn_sinks, w_out, final_g):
    bp, seq, _ = x_prompt.shape
    bs = x_sample.shape[0]
    past_len = 8192
    assert seq % (CHUNK * 128) == 0 and x_sample.shape[1] == 1 and cache_win_k.shape[2] == WINDOW
    assert bp + bs <= MOD_ROWS

    w_in_b = w_in.astype(BF16)
    w_glu_b = w_glu.astype(BF16)
    w_out_b4 = w_out.astype(BF16).reshape(DEPTH, 2, D_SSM, D_MODEL)
    norm_g3 = norm_g.reshape(DEPTH, 1, D_MODEL)
    b_glu3 = b_glu.reshape(DEPTH, 1, D_SSM)
    ssm_d3 = ssm_d.reshape(DEPTH, 1, D_SSM)

    c_all = jnp.concatenate([c_prompt, c_sample, jnp.zeros((MOD_ROWS - bp - bs, D_MODEL), F32)], axis=0)
    mod = _modulation(c_all, w_ada, b_ada)
    mod_p = mod[:, :bp].reshape(DEPTH, bp, 1, 3 * D_MODEL)
    mod_s = mod[:, bp:bp + bs].reshape(DEPTH, 1, bs, 3 * D_MODEL)

    prep = _ssm_prep(ssm_a_re, ssm_a_im, ssm_log_dt, ssm_b_re, ssm_b_im, ssm_c_re, ssm_c_im)
    cos_p, sin_p = _rope_tables(jnp.arange(seq))
    cos_s, sin_s = _rope_tables(jnp.full((bs,), past_len))

    xcur = [(x_prompt, k) for k in range(bp)]
    pk = [[None] * bp for _ in range(DEPTH)]
    pv = [[None] * bp for _ in range(DEPTH)]
    pr = [[None] * bp for _ in range(DEPTH)]
    pi = [[None] * bp for _ in range(DEPTH)]

    hnext = [None] * bp

    def finish(z, l, k, sa):
        yg, hr, hi, oa = sa
        x_new, *h = _mix_out(yg, z, oa, w_glu_b, b_glu3, w_out_b4, xcur[k][0], xcur[k][1], mod_p, k, l,
                             norm_g3)
        xcur[k] = (x_new[None], 0)
        hnext[k] = h[0] if h else None
        pk[l][k] = z[seq - WINDOW:, Z_K:Z_K + D_KV].astype(F32).reshape(WINDOW, N_KV_HEADS, HEAD_DIM)
        pv[l][k] = z[seq - WINDOW:, Z_V:Z_V + D_KV].astype(F32).reshape(WINDOW, N_KV_HEADS, HEAD_DIM)
        pr[l][k] = hr.reshape(N_GROUPS, SSM_STATE)
        pi[l][k] = hi.reshape(N_GROUPS, SSM_STATE)

    pending = None
    for l in range(DEPTH):
        for k in range(bp):
            h = _rms_mod(xcur[k][0], xcur[k][1], norm_g3, mod_p, k, l, 256) if l == 0 else hnext[k]
            if pending is None:
                z, = _prompt_stage(h, l, w_in_b, cos_p, sin_p, None, None, prep, ssm_d3, attn_sinks)
            else:
                z, *sa = _prompt_stage(h, l, w_in_b, cos_p, sin_p, pending[0], pending[1], prep, ssm_d3,
                                       attn_sinks)
                finish(*pending, sa)
            pending = (z, l, k)
    sa = _prompt_stage(None, None, w_in_b, cos_p, sin_p, pending[0], pending[1], prep, ssm_d3, attn_sinks)
    finish(*pending, sa)
    y_prompt = _final_norm_seqs([x[0] for x, _ in xcur], final_g.reshape(1, D_MODEL), 256)
    pk, pv, pr, pi = (jnp.stack([jnp.stack(row) for row in t]) for t in (pk, pv, pr, pi))

    xs = x_sample.reshape(bs, D_MODEL)
    sk, sv, sr, si = [], [], [], []
    for l in range(DEPTH):
        h = _rms_mod(xs[None], 0, norm_g3, mod_s, 0, l, bs)
        u, zb = _in_proj(h, w_in_b, cos_s, sin_s, l, bs)
        ut = u.reshape(bs, N_GROUPS, SSM_GROUP).transpose(1, 2, 0)
        ht_r = state_ssm_re[l].transpose(1, 2, 0)
        ht_i = state_ssm_im[l].transpose(1, 2, 0)
        ygt, nhr, nhi = _ssm_step(ut, ht_r, ht_i, prep, ssm_b_re, ssm_b_im, ssm_c_re, ssm_c_im, ssm_d, l)
        yg = ygt.transpose(2, 0, 1).reshape(bs, D_SSM).astype(BF16)
        zf = zb.astype(F32)
        q3 = zf[:, ZB_Q:ZB_Q + D_ATT].reshape(bs, N_HEADS, HEAD_DIM)
        ga3 = zf[:, ZB_GA:ZB_GA + D_ATT].reshape(bs, N_HEADS, HEAD_DIM)
        kn = zf[:, ZB_K:ZB_K + D_KV].reshape(bs, N_KV_HEADS, HEAD_DIM)
        vn = zf[:, ZB_V:ZB_V + D_KV].reshape(bs, N_KV_HEADS, HEAD_DIM)
        oa = _attention_step(q3, ga3, kn, vn, cache_win_k, cache_win_v, attn_sinks, l)
        ys = _glu(yg, zb, ZB_GS, w_glu_b, b_glu3, l, bs)
        xs = _out_proj(ys, oa.reshape(bs, D_ATT).astype(BF16), w_out_b4, xs[None], 0, mod_s, 0, l, bs)
        sk.append(kn.reshape(bs, 1, N_KV_HEADS, HEAD_DIM))
        sv.append(vn.reshape(bs, 1, N_KV_HEADS, HEAD_DIM))
        sr.append(nhr.transpose(2, 0, 1))
        si.append(nhi.transpose(2, 0, 1))
    y_sample = _final_norm(xs, final_g.reshape(1, D_MODEL), bs).reshape(bs, 1, D_MODEL)
    new_k, new_v = _roll_cache(cache_win_k, cache_win_v, jnp.stack(sk), jnp.stack(sv))

    return (y_prompt, y_sample, jnp.stack(pk), jnp.stack(pv), jnp.stack(pr), jnp.stack(pi),
            new_k, new_v, jnp.stack(sr), jnp.stack(si))
```

```python
import functools
import math

import jax
import jax.numpy as jnp
from jax import lax
from jax.experimental import pallas as pl
from jax.experimental.pallas import tpu as pltpu

F32 = jnp.float32
BF16 = jnp.bfloat16

D_MODEL = 4096
DEPTH = 4
D_SSM = 2048
D_ATT = 2048
SSM_GROUP = 16
N_GROUPS = 128
SSM_STATE = 64
HEAD_DIM = 128
N_HEADS = 16
N_KV_HEADS = 4
KV_GROUP = 4
D_KV = 512
WINDOW = 128
ROPE_THETA = 10000.0
EPS = 1e-6
BLOCK = 128
D_IN = 9216
ATT_SCALE = HEAD_DIM ** -0.5

IN_TN = 512
IN_TILES = D_IN // IN_TN
U_TILES = D_SSM // IN_TN
ZB_COLS = D_IN - D_SSM
ZB_Q, ZB_GA, ZB_GS, ZB_K, ZB_V = 0, 2048, 4096, 6144, 6656

CHUNK = 16
SSM_GB = 8
SSM_W = SSM_GB * SSM_GROUP
SCAN_LANES = 128
MOD_ROWS = 40

VMEM_LIMIT_BYTES = 56 * 1024 * 1024


def _params(*sem):
    return pltpu.CompilerParams(dimension_semantics=sem, vmem_limit_bytes=VMEM_LIMIT_BYTES)


def _mod_body(c_ref, w_ref, b_ref, o_ref):
    a = jax.nn.silu(c_ref[...]).astype(BF16)
    o_ref[...] = jnp.dot(a, w_ref[...].astype(BF16), preferred_element_type=F32) + b_ref[...]


def _modulation(c_all, w_ada, b_ada):
    tn = 512
    return pl.pallas_call(
        _mod_body,
        grid=(DEPTH, 3 * D_MODEL // tn),
        in_specs=[pl.BlockSpec((MOD_ROWS, D_MODEL), lambda l, j: (0, 0)),
                  pl.BlockSpec((None, D_MODEL, tn), lambda l, j: (l, 0, j)),
                  pl.BlockSpec((None, 1, tn), lambda l, j: (l, 0, j))],
        out_specs=pl.BlockSpec((None, MOD_ROWS, tn), lambda l, j: (l, 0, j)),
        out_shape=jax.ShapeDtypeStruct((DEPTH, MOD_ROWS, 3 * D_MODEL), F32),
        compiler_params=_params("arbitrary", "arbitrary"),
        name="adaln_mod",
    )(c_all, w_ada, b_ada.reshape(DEPTH, 1, 3 * D_MODEL))


def _rmsmod_body(x_ref, g_ref, sc_ref, sh_ref, o_ref):
    x = x_ref[...]
    y = x * lax.rsqrt(jnp.mean(x * x, axis=-1, keepdims=True) + EPS)
    o_ref[...] = ((y * g_ref[...]) * (1.0 + sc_ref[...]) + sh_ref[...]).astype(BF16)


def _rms_mod(x3, kx, norm_g3, mod4, kb, l, tm):
    lr = x3.shape[1]
    r = mod4.shape[2]
    return pl.pallas_call(
        _rmsmod_body,
        grid=(lr // tm,),
        in_specs=[pl.BlockSpec((None, tm, D_MODEL), lambda i: (kx, i, 0)),
                  pl.BlockSpec((None, 1, D_MODEL), lambda i: (l, 0, 0)),
                  pl.BlockSpec((None, None, r, D_MODEL), lambda i: (l, kb, 0, 1)),
                  pl.BlockSpec((None, None, r, D_MODEL), lambda i: (l, kb, 0, 0))],
        out_specs=pl.BlockSpec((tm, D_MODEL), lambda i: (i, 0)),
        out_shape=jax.ShapeDtypeStruct((lr, D_MODEL), BF16),
        compiler_params=_params("arbitrary"),
        name="rms_mod",
    )(x3, norm_g3, mod4, mod4)


def _w_in_tile(j):
    return jnp.where(j < 4, j, jnp.where(j < 8, j + 4, jnp.where(j < 12, j + 6, jnp.where(j < 16, j - 8, j - 4))))


def _rope(x, cos, sin):
    return x * cos + pltpu.roll(x, HEAD_DIM // 2, 1) * sin


def _inproj_body(h_ref, w_ref, cos_ref, sin_ref, u_ref, zb_ref):
    j = pl.program_id(1)
    half = IN_TN // 2

    def halves(epilogue):
        for c in range(2):
            sl = slice(c * half, (c + 1) * half)
            epilogue(sl, jnp.dot(h_ref[...], w_ref[:, sl], preferred_element_type=F32))

    @pl.when(j < 4)
    def _():
        def plain(sl, acc):
            u_ref[:, sl] = acc
        halves(plain)

    @pl.when(((j >= 4) & (j < 8)) | (j == 16))
    def _():
        def rope(sl, acc):
            for hh in range(half // HEAD_DIM):
                hs = slice(hh * HEAD_DIM, (hh + 1) * HEAD_DIM)
                zb_ref[:, sl.start + hs.start:sl.start + hs.stop] = _rope(
                    acc[:, hs], cos_ref[...], sin_ref[...]).astype(BF16)
        halves(rope)

    @pl.when((j >= 8) & (j < 16))
    def _():
        def silu(sl, acc):
            zb_ref[:, sl] = jax.nn.silu(acc).astype(BF16)
        halves(silu)

    @pl.when(j == 17)
    def _():
        def cast(sl, acc):
            zb_ref[:, sl] = acc.astype(BF16)
        halves(cast)


def _in_proj(h2, w_in_b, cos_t, sin_t, l, tm):
    m = h2.shape[0]
    nt = cos_t.shape[0] // tm
    return pl.pallas_call(
        _inproj_body,
        grid=(m // tm, IN_TILES),
        in_specs=[pl.BlockSpec((tm, D_MODEL), lambda i, j: (i, 0)),
                  pl.BlockSpec((None, D_MODEL, IN_TN), lambda i, j: (l, 0, _w_in_tile(j))),
                  pl.BlockSpec((tm, HEAD_DIM), lambda i, j: (i % nt, 0)),
                  pl.BlockSpec((tm, HEAD_DIM), lambda i, j: (i % nt, 0))],
        out_specs=[pl.BlockSpec((tm, IN_TN), lambda i, j: (i, jnp.minimum(j, U_TILES - 1))),
                   pl.BlockSpec((tm, IN_TN), lambda i, j: (i, jnp.maximum(j - U_TILES, 0)))],
        out_shape=[jax.ShapeDtypeStruct((m, D_SSM), F32),
                   jax.ShapeDtypeStruct((m, ZB_COLS), BF16)],
        compiler_params=_params("arbitrary", "arbitrary"),
        name="in_proj",
    )(h2, w_in_b, cos_t, sin_t)


def _cmul(ar, ai, br, bi):
    return ar * br - ai * bi, ar * bi + ai * br


def _discretise(ar, ai, log_dt):
    dt = jnp.exp(log_dt)
    mag = jnp.exp(ar * dt)
    lr = mag * jnp.cos(ai * dt)
    li = mag * jnp.sin(ai * dt)
    nr = lr - 1.0
    ni = li
    den = ar * ar + ai * ai
    qr = (nr * ar + ni * ai) / den
    qi = (ni * ar - nr * ai) / den
    return lr, li, qr, qi


def _powers(lr, li, n):
    pr, pi = [jnp.ones_like(lr)], [jnp.zeros_like(li)]
    for _ in range(n):
        r, i = _cmul(pr[-1], pi[-1], lr, li)
        pr.append(r)
        pi.append(i)
    return pr, pi


PREP_GB = 16


def _split_bf16(x):
    hi = x.astype(BF16)
    return hi, (x - hi.astype(F32)).astype(BF16)


def _ssm_prep_body(a2r_ref, a2i_ref, ldt_ref, cc_ref, cs_ref, bc_ref, bs_ref,
                   mt_ref, ws_ref, wc_ref, l16r_ref, l16i_ref, l1r_ref, l1i_ref, qr_ref, qi_ref):
    gb, t_len, p = PREP_GB, CHUNK, SSM_STATE
    kdim = t_len * SSM_GROUP
    lr, li, qr, qi = _discretise(a2r_ref[...], a2i_ref[...], ldt_ref[...])
    sgn = jnp.where(lax.broadcasted_iota(jnp.int32, (1, 2 * p), 1) < p, 1.0, -1.0)
    pr, pi = _powers(lr, li, t_len)
    mag2 = lr * lr + li * li
    ir, ii = _powers(lr / mag2, -li / mag2, t_len - 1)
    prs = [x * sgn for x in pr]
    w_inv = [_cmul(qr, qi, ir[s], ii[s]) for s in range(t_len)]
    w_15 = [_cmul(qr, qi, pr[t_len - 1 - s], pi[t_len - 1 - s]) for s in range(t_len)]
    w_inv = [(wr, wi * sgn) for wr, wi in w_inv]
    w_15 = [(wr, wi * sgn) for wr, wi in w_15]

    row_t = lax.broadcasted_iota(jnp.int32, (kdim, kdim), 0) >> 4
    col_s = lax.broadcasted_iota(jnp.int32, (kdim, kdim), 1) >> 4
    causal = row_t >= col_s
    nt = (((1,), (1,)), ((), ()))
    for g in range(gb):
        row = slice(g, g + 1)
        cc, cs = cc_ref[g], cs_ref[g]
        bc, bs = bc_ref[g], bs_ref[g]
        cl = jnp.concatenate([cc * prs[t][row] - cs * pi[t][row] for t in range(t_len)], axis=0)
        wc = jnp.concatenate([cc * prs[t + 1][row] - cs * pi[t + 1][row] for t in range(t_len)], axis=0)
        bl = jnp.concatenate([bc * w_inv[s][0][row] - bs * w_inv[s][1][row] for s in range(t_len)], axis=0)
        ws = jnp.concatenate([bc * w_15[s][0][row] - bs * w_15[s][1][row] for s in range(t_len)], axis=0)
        a_hi, a_lo = _split_bf16(cl)
        b_hi, b_lo = _split_bf16(bl)
        m = (lax.dot_general(a_hi, b_hi, nt, preferred_element_type=F32)
             + lax.dot_general(a_hi, b_lo, nt, preferred_element_type=F32)
             + lax.dot_general(a_lo, b_hi, nt, preferred_element_type=F32))
        mt_ref[g] = jnp.where(causal, m, 0.0).astype(BF16)
        ws_ref[g] = ws.T.astype(BF16)
        wc_ref[g] = wc.astype(BF16)

    pad = jnp.zeros((2 * p - gb, 2 * p), F32)
    outs = ((pr[t_len], l16r_ref), (pi[t_len], l16i_ref), (lr, l1r_ref), (li, l1i_ref),
            (qr, qr_ref), (qi, qi_ref))
    for val, ref in outs:
        col = jnp.concatenate([val, pad], axis=0).T
        for g in range(gb):
            ref[g] = jnp.broadcast_to(col[:p, g:g + 1], ref.shape[1:])


def _ssm_prep(a_re, a_im, log_dt, b_re, b_im, c_re, c_im):
    g, p, kdim, gb = N_GROUPS, SSM_STATE, CHUNK * SSM_GROUP, PREP_GB
    a2r = jnp.concatenate([a_re, a_re], axis=-1)
    a2i = jnp.concatenate([a_im, a_im], axis=-1)
    ldt = log_dt.reshape(DEPTH, g, 1)
    bt_re = jnp.swapaxes(b_re, -1, -2)
    bt_im = jnp.swapaxes(b_im, -1, -2)
    cc = jnp.concatenate([c_re, c_im], axis=-1)
    cs = jnp.concatenate([c_im, c_re], axis=-1)
    bc = jnp.concatenate([bt_re, bt_im], axis=-1)
    bs = jnp.concatenate([bt_im, bt_re], axis=-1)

    def spec(a, b):
        return pl.BlockSpec((None, gb, a, b), lambda l, i: (l, i, 0, 0))

    def spec2(b):
        return pl.BlockSpec((None, gb, b), lambda l, i: (l, i, 0))

    col_shape = jax.ShapeDtypeStruct((DEPTH, g, p, 1), F32)
    wide_shape = jax.ShapeDtypeStruct((DEPTH, g, p, SCAN_LANES), F32)
    return pl.pallas_call(
        _ssm_prep_body,
        grid=(DEPTH, g // gb),
        in_specs=[spec2(2 * p), spec2(2 * p), spec2(1),
                  spec(SSM_GROUP, 2 * p), spec(SSM_GROUP, 2 * p), spec(SSM_GROUP, 2 * p),
                  spec(SSM_GROUP, 2 * p)],
        out_specs=([spec(kdim, kdim), spec(2 * p, kdim), spec(kdim, 2 * p)]
                   + [spec(p, SCAN_LANES)] * 2 + [spec(p, 1)] * 4),
        out_shape=[jax.ShapeDtypeStruct((DEPTH, g, kdim, kdim), BF16),
                   jax.ShapeDtypeStruct((DEPTH, g, 2 * p, kdim), BF16),
                   jax.ShapeDtypeStruct((DEPTH, g, kdim, 2 * p), BF16)] + [wide_shape] * 2 + [col_shape] * 4,
        compiler_params=_params("arbitrary", "arbitrary"),
        name="ssm_prep",
    )(a2r, a2i, ldt, cc, cs, bc, bs)


def _ssm_part(u_ref, mt_ref, ws_ref, wc_ref, l16r_ref, l16i_ref, d_ref,
              yg_ref, hr_ref, hi_ref, ubuf, ybuf):
    gb, t_len, p = SSM_GB, CHUNK, SSM_STATE
    r_len = u_ref.shape[0] // t_len
    ubuf[...] = u_ref[...].astype(F32)
    slab_t = [ubuf[pl.ds(s, r_len, stride=t_len), :].T.astype(BF16) for s in range(t_len)]

    yl, xr, xi = [], [], []
    for g in range(gb):
        rows = slice(g * SSM_GROUP, (g + 1) * SSM_GROUP)
        s_g = jnp.concatenate([st[rows, :] for st in slab_t], axis=0)
        both = jnp.dot(jnp.concatenate([mt_ref[g], ws_ref[g]], axis=0), s_g, preferred_element_type=F32)
        kd = t_len * SSM_GROUP
        yl.append(both[:kd, :])
        xr.append(both[kd:kd + p, :])
        xi.append(both[kd + p:, :])
    h_r = jnp.concatenate(xr, axis=0)
    h_i = jnp.concatenate(xi, axis=0)
    yield

    a_r = l16r_ref[...].reshape(gb * p, r_len)
    a_i = l16i_ref[...].reshape(gb * p, r_len)
    lane = lax.broadcasted_iota(jnp.int32, (gb * p, r_len), 1)
    k = 1
    while k < r_len:
        keep = lane >= k
        s_r = jnp.where(keep, pltpu.roll(h_r, k, 1), 0.0)
        s_i = jnp.where(keep, pltpu.roll(h_i, k, 1), 0.0)
        h_r, h_i = h_r + a_r * s_r - a_i * s_i, h_i + a_r * s_i + a_i * s_r
        a_r, a_i = a_r * a_r - a_i * a_i, 2.0 * a_r * a_i
        k *= 2
    hr_ref[...] = h_r[:, r_len - 1:r_len].reshape(gb, p, 1)
    hi_ref[...] = h_i[:, r_len - 1:r_len].reshape(gb, p, 1)

    first = lane >= 1
    c_r = jnp.where(first, pltpu.roll(h_r, 1, 1), 0.0).astype(BF16)
    c_i = jnp.where(first, pltpu.roll(h_i, 1, 1), 0.0).astype(BF16)
    ys = []
    for g in range(gb):
        rows = slice(g * p, (g + 1) * p)
        c_g = jnp.concatenate([c_r[rows, :], c_i[rows, :]], axis=0)
        ys.append(yl[g] + jnp.dot(wc_ref[g], c_g, preferred_element_type=F32))
    yield
    d = d_ref[...]
    for t in range(t_len):
        rows = slice(t * SSM_GROUP, (t + 1) * SSM_GROUP)
        y_t = jnp.concatenate([y[rows, :] for y in ys], axis=0)
        u_t = ubuf[pl.ds(t, r_len, stride=t_len), :]
        ybuf[pl.ds(t, r_len, stride=t_len), :] = jax.nn.gelu(y_t.T + d * u_t)
    yg_ref[...] = ybuf[...].astype(BF16)


SSM_STEP_GB = 16


def _ssm_step_body(lr_ref, li_ref, qr_ref, qi_ref, br_ref, bi_ref, cr_ref, ci_ref, d_ref, ut_ref,
                   hr_ref, hi_ref, yt_ref, nhr_ref, nhi_ref):
    lr, li = lr_ref[...], li_ref[...]
    bbr, bbi = _cmul(qr_ref[...], qi_ref[...], br_ref[...], bi_ref[...])
    for g in range(SSM_STEP_GB):
        u = ut_ref[g]
        ub = u.astype(BF16)
        x_r = jnp.dot(bbr[g].astype(BF16), ub, preferred_element_type=F32)
        x_i = jnp.dot(bbi[g].astype(BF16), ub, preferred_element_type=F32)
        h_r, h_i = hr_ref[g], hi_ref[g]
        n_r = lr[g] * h_r - li[g] * h_i + x_r
        n_i = lr[g] * h_i + li[g] * h_r + x_i
        y = (jnp.dot(cr_ref[g].astype(BF16), n_r.astype(BF16), preferred_element_type=F32)
             - jnp.dot(ci_ref[g].astype(BF16), n_i.astype(BF16), preferred_element_type=F32))
        yt_ref[g] = jax.nn.gelu(y + d_ref[g] * u)
        nhr_ref[g] = n_r
        nhi_ref[g] = n_i


def _ssm_step(ut, ht_r, ht_i, prep, b_re, b_im, c_re, c_im, ssm_d, l):
    l1r, l1i, qr, qi = prep[5:9]
    g, p, gb = N_GROUPS, SSM_STATE, SSM_STEP_GB
    bs = ut.shape[-1]

    def lspec(a, c):
        return pl.BlockSpec((None, gb, a, c), lambda i: (l, i, 0, 0))

    def aspec(a):
        return pl.BlockSpec((gb, a, bs), lambda i: (i, 0, 0))

    return pl.pallas_call(
        _ssm_step_body,
        grid=(g // gb,),
        in_specs=[lspec(p, 1), lspec(p, 1), lspec(p, 1), lspec(p, 1), lspec(p, SSM_GROUP), lspec(p, SSM_GROUP),
                  lspec(SSM_GROUP, p), lspec(SSM_GROUP, p), lspec(SSM_GROUP, 1),
                  aspec(SSM_GROUP), aspec(p), aspec(p)],
        out_specs=[aspec(SSM_GROUP), aspec(p), aspec(p)],
        out_shape=[jax.ShapeDtypeStruct((g, SSM_GROUP, bs), F32),
                   jax.ShapeDtypeStruct((g, p, bs), F32),
                   jax.ShapeDtypeStruct((g, p, bs), F32)],
        compiler_params=_params("arbitrary"),
        name="ssm_step",
    )(l1r, l1i, qr, qi, b_re, b_im, c_re, c_im, ssm_d.reshape(DEPTH, g, SSM_GROUP, 1), ut, ht_r, ht_i)


def _attn_part(l, n, q_ref, ga_ref, kc_ref, kp_ref, vc_ref, vp_ref, sink_ref, o_ref):
    cols = KV_GROUP * BLOCK
    j = lax.broadcasted_iota(jnp.int32, (2 * BLOCK, cols), 0)
    r = lax.broadcasted_iota(jnp.int32, (2 * BLOCK, cols), 1) & (BLOCK - 1)
    no_prev = jnp.where(n > 0, 0, 2 * BLOCK)
    valid = ((j < BLOCK) & (j > r + no_prev)) | ((j >= BLOCK) & (j - BLOCK <= r))
    ones = jnp.ones((HEAD_DIM, 2 * BLOCK), BF16)
    nt = (((1,), (1,)), ((), ()))
    heads = [[kh * KV_GROUP + g for g in range(KV_GROUP)] for kh in range(N_KV_HEADS)]
    scores = []
    for kh in range(N_KV_HEADS):
        ksl = slice(kh * HEAD_DIM, (kh + 1) * HEAD_DIM)
        k = jnp.concatenate([kp_ref[:, ksl], kc_ref[:, ksl]], axis=0)
        q4 = jnp.concatenate([q_ref[:, h * HEAD_DIM:(h + 1) * HEAD_DIM] for h in heads[kh]], axis=0)
        scores.append(lax.dot_general(k, q4, nt, preferred_element_type=F32))
    yield
    outs = []
    for kh in range(N_KV_HEADS):
        ksl = slice(kh * HEAD_DIM, (kh + 1) * HEAD_DIM)
        vt = jnp.concatenate([vp_ref[:, ksl].T, vc_ref[:, ksl].T], axis=1)
        v1t = jnp.concatenate([vt, ones], axis=0)
        st = jnp.where(valid, scores[kh] * ATT_SCALE, -jnp.inf)
        sink = jnp.concatenate([jnp.full((1, BLOCK), sink_ref[l, h], F32) for h in heads[kh]], axis=1)
        m = jnp.maximum(jnp.max(st, axis=0, keepdims=True), sink)
        e = jnp.exp(st - m).astype(BF16)
        o1 = jnp.dot(v1t, e, preferred_element_type=F32)
        outs.append((o1, jnp.exp(sink - m)))
    yield
    for kh in range(N_KV_HEADS):
        o1, e_sink = outs[kh]
        ot = o1[:HEAD_DIM, :] / (o1[HEAD_DIM:, :] + e_sink)
        for g, h in enumerate(heads[kh]):
            hsl = slice(h * HEAD_DIM, (h + 1) * HEAD_DIM)
            o_ref[:, hsl] = (ot[:, g * BLOCK:(g + 1) * BLOCK].T * ga_ref[:, hsl].astype(F32)).astype(BF16)


Z_Q, Z_GA, Z_GS, Z_K, Z_V = 2048, 4096, 6144, 8192, 8704
SA_STEPS = N_GROUPS // SSM_GB
IN_ROWS = 512


def _in_part(j, h_ref, w_ref, cos_ref, sin_ref, z_ref):
    is_rope = ((j >= 4) & (j < 8)) | (j == 16)
    is_silu = (j >= 8) & (j < 16)
    for rc in range(h_ref.shape[0] // IN_ROWS):
        rs = slice(rc * IN_ROWS, (rc + 1) * IN_ROWS)
        acc = jnp.dot(h_ref[rs, :], w_ref[...], preferred_element_type=F32)
        cos, sin = cos_ref[rs, :], sin_ref[rs, :]
        roped = jnp.concatenate([_rope(acc[:, hh * HEAD_DIM:(hh + 1) * HEAD_DIM], cos, sin)
                                 for hh in range(IN_TN // HEAD_DIM)], axis=1)
        out = jnp.where(is_rope, roped, jnp.where(is_silu, jax.nn.silu(acc), acc))
        z_ref[rs, :] = out.astype(BF16)
        yield


def _stage_body(has_in, has_sa, l_prev, *refs):
    j = pl.program_id(0)
    refs = list(refs)
    in_refs = [refs.pop(0) for _ in range(4)] if has_in else []
    sa_refs = [refs.pop(0) for _ in range(14)] if has_sa else []
    proj = _in_part(j, *in_refs, refs.pop(0)) if has_in else iter(())
    side = []
    if has_sa:
        u_ref, q_ref, ga_ref, kc_ref, kp_ref, vc_ref, vp_ref = sa_refs[:7]
        mt_ref, ws_ref, wc_ref, l16r_ref, l16i_ref, d_ref, sink_ref = sa_refs[7:]
        yg_ref, hr_ref, hi_ref, oa_ref, ubuf, ybuf = refs
        side = [_ssm_part(u_ref, mt_ref, ws_ref, wc_ref, l16r_ref, l16i_ref, d_ref, yg_ref, hr_ref,
                          hi_ref, ubuf, ybuf),
                _attn_part(l_prev, jnp.minimum(j, SA_STEPS - 1), q_ref, ga_ref, kc_ref, kp_ref, vc_ref,
                           vp_ref, sink_ref, oa_ref)]
    live = True
    while live:
        live = False
        for gen in side + [proj]:
            if next(gen, "done") != "done":
                live = True


def _prompt_stage(h, l_in, w_in_b, cos_t, sin_t, z_prev, l_prev, prep, ssm_d3, sinks):
    has_in, has_sa = h is not None, z_prev is not None
    seq = (h if has_in else z_prev).shape[0]
    gb, p, kdim = SSM_GB, SSM_STATE, CHUNK * SSM_GROUP
    in_specs, args, out_specs, out_shape, scratch = [], [], [], [], []

    def jm(j):
        return jnp.minimum(j, SA_STEPS - 1)

    if has_in:
        table = pl.BlockSpec((seq, HEAD_DIM), lambda j: (0, 0))
        in_specs += [pl.BlockSpec((seq, D_MODEL), lambda j: (0, 0), pipeline_mode=pl.Buffered(1)),
                     pl.BlockSpec((None, D_MODEL, IN_TN), lambda j: (l_in, 0, _w_in_tile(j))), table, table]
        args += [h, w_in_b, cos_t, sin_t]
        out_specs.append(pl.BlockSpec((seq, IN_TN), lambda j: (0, j)))
        out_shape.append(jax.ShapeDtypeStruct((seq, D_IN), BF16))
    if has_sa:
        def wspec(a, c):
            return pl.BlockSpec((None, gb, a, c), lambda j: (l_prev, jm(j), 0, 0))

        def kv(col, prev):
            if prev:
                return pl.BlockSpec((BLOCK, D_KV), lambda j: (jnp.maximum(jm(j) - 1, 0), col // D_KV))
            return pl.BlockSpec((BLOCK, D_KV), lambda j: (jm(j), col // D_KV))

        st_spec = pl.BlockSpec((gb, p, 1), lambda j: (jm(j), 0, 0))
        in_specs += [pl.BlockSpec((seq, SSM_W), lambda j: (0, jm(j))),
                     pl.BlockSpec((BLOCK, D_ATT), lambda j: (jm(j), Z_Q // D_ATT)),
                     pl.BlockSpec((BLOCK, D_ATT), lambda j: (jm(j), Z_GA // D_ATT)),
                     kv(Z_K, False), kv(Z_K, True), kv(Z_V, False), kv(Z_V, True),
                     wspec(kdim, kdim), wspec(2 * p, kdim), wspec(kdim, 2 * p),
                     wspec(p, SCAN_LANES), wspec(p, SCAN_LANES),
                     pl.BlockSpec((None, 1, SSM_W), lambda j: (l_prev, 0, jm(j))),
                     pl.BlockSpec(memory_space=pltpu.SMEM)]
        args += [z_prev] * 7 + list(prep[:5]) + [ssm_d3, sinks]
        out_specs += [pl.BlockSpec((seq, SSM_W), lambda j: (0, jm(j))), st_spec, st_spec,
                      pl.BlockSpec((BLOCK, D_ATT), lambda j: (jm(j), 0))]
        out_shape += [jax.ShapeDtypeStruct((seq, D_SSM), BF16),
                      jax.ShapeDtypeStruct((N_GROUPS, p, 1), F32),
                      jax.ShapeDtypeStruct((N_GROUPS, p, 1), F32),
                      jax.ShapeDtypeStruct((seq, D_ATT), BF16)]
        scratch = [pltpu.VMEM((seq, SSM_W), F32), pltpu.VMEM((seq, SSM_W), F32)]
    return pl.pallas_call(
        functools.partial(_stage_body, has_in, has_sa, l_prev),
        grid=(IN_TILES if has_in else SA_STEPS,),
        in_specs=in_specs, out_specs=out_specs, out_shape=out_shape, scratch_shapes=scratch,
        compiler_params=_params("arbitrary"),
        name="stage" + ("_in" if has_in else "") + ("_sa" if has_sa else ""),
    )(*args)


STEP_SB = 8


def _attn_step_body(l, q_ref, ga_ref, kn_ref, vn_ref, ck_ref, cv_ref, sink_ref, o_ref):
    rows = WINDOW * N_KV_HEADS
    head = lax.broadcasted_iota(jnp.int32, (N_HEADS, rows), 0)
    col = lax.broadcasted_iota(jnp.int32, (N_HEADS, rows), 1)
    valid = ((col & (N_KV_HEADS - 1)) == (head >> 2)) & (col >= N_KV_HEADS)
    sink = jnp.concatenate([jnp.full((1, 1), sink_ref[l, h], F32) for h in range(N_HEADS)], axis=0)
    nt = (((1,), (1,)), ((), ()))

    def per_head(x4):
        return jnp.concatenate([jnp.broadcast_to(x4[kh:kh + 1, :], (KV_GROUP, HEAD_DIM))
                                for kh in range(N_KV_HEADS)], axis=0)

    for b in range(STEP_SB):
        q = q_ref[b].astype(BF16)
        kn = per_head(kn_ref[b]).astype(BF16).astype(F32)
        vn = per_head(vn_ref[b]).astype(BF16).astype(F32)
        s = lax.dot_general(q, ck_ref[b].astype(BF16), nt, preferred_element_type=F32) * ATT_SCALE
        s = jnp.where(valid, s, -jnp.inf)
        s_new = jnp.sum(q.astype(F32) * kn, axis=-1, keepdims=True) * ATT_SCALE
        m = jnp.maximum(jnp.maximum(jnp.max(s, axis=-1, keepdims=True), s_new), sink)
        e = jnp.exp(s - m)
        e_new = jnp.exp(s_new - m)
        den = jnp.sum(e, axis=-1, keepdims=True) + e_new + jnp.exp(sink - m)
        o = (jnp.dot((e / den).astype(BF16), cv_ref[b].astype(BF16), preferred_element_type=F32)
             + (e_new / den).astype(BF16).astype(F32) * vn)
        o_ref[b] = o * ga_ref[b]


def _attention_step(q3, ga3, kn3, vn3, cache_k, cache_v, sinks, l):
    bs = q3.shape[0]
    sb = STEP_SB
    rows = WINDOW * N_KV_HEADS
    hspec = pl.BlockSpec((sb, N_HEADS, HEAD_DIM), lambda i: (i, 0, 0))
    nspec = pl.BlockSpec((sb, N_KV_HEADS, HEAD_DIM), lambda i: (i, 0, 0))
    cspec = pl.BlockSpec((None, sb, rows, HEAD_DIM), lambda i: (l, i, 0, 0))
    ck = cache_k.reshape(DEPTH, bs, rows, HEAD_DIM)
    cv = cache_v.reshape(DEPTH, bs, rows, HEAD_DIM)
    return pl.pallas_call(
        functools.partial(_attn_step_body, l),
        grid=(bs // sb,),
        in_specs=[hspec, hspec, nspec, nspec, cspec, cspec, pl.BlockSpec(memory_space=pltpu.SMEM)],
        out_specs=hspec,
        out_shape=jax.ShapeDtypeStruct((bs, N_HEADS, HEAD_DIM), F32),
        compiler_params=_params("arbitrary"),
        name="swa_step",
    )(q3, ga3, kn3, vn3, ck, cv, sinks)


def _roll_cache_body(ck_ref, cv_ref, kn_ref, vn_ref, ok_ref, ov_ref):
    for c_ref, n_ref, o_ref in ((ck_ref, kn_ref, ok_ref), (cv_ref, vn_ref, ov_ref)):
        o_ref[:, :WINDOW - 1] = c_ref[:, 1:]
        o_ref[:, WINDOW - 1:] = n_ref[...]


def _roll_cache(cache_k, cache_v, k_new, v_new):
    bs = cache_k.shape[1]
    sb = STEP_SB
    cspec = pl.BlockSpec((None, sb, WINDOW, N_KV_HEADS, HEAD_DIM), lambda l, i: (l, i, 0, 0, 0))
    nspec = pl.BlockSpec((None, sb, 1, N_KV_HEADS, HEAD_DIM), lambda l, i: (l, i, 0, 0, 0))
    shape = jax.ShapeDtypeStruct(cache_k.shape, F32)
    return pl.pallas_call(
        _roll_cache_body,
        grid=(DEPTH, bs // sb),
        in_specs=[cspec, cspec, nspec, nspec],
        out_specs=[cspec, cspec],
        out_shape=[shape, shape],
        compiler_params=_params("arbitrary", "arbitrary"),
        name="roll_cache",
    )(cache_k, cache_v, k_new, v_new)


def _glu_body(y_ref, w_ref, b_ref, gs_ref, o_ref):
    y = y_ref[...]
    step = 512
    for c in range(D_SSM // step):
        sl = slice(c * step, (c + 1) * step)
        acc = jnp.dot(y, w_ref[:, sl], preferred_element_type=F32) + b_ref[:, sl]
        o_ref[:, sl] = (y_ref[:, sl].astype(F32) * jax.nn.sigmoid(acc)
                        * gs_ref[:, sl].astype(F32)).astype(BF16)


def _glu(yg2, zb2, gs_col, w_glu_b, b_glu3, l, tm):
    m = yg2.shape[0]
    return pl.pallas_call(
        _glu_body,
        grid=(m // tm,),
        in_specs=[pl.BlockSpec((tm, D_SSM), lambda i: (i, 0)),
                  pl.BlockSpec((None, D_SSM, D_SSM), lambda i: (l, 0, 0)),
                  pl.BlockSpec((None, 1, D_SSM), lambda i: (l, 0, 0)),
                  pl.BlockSpec((tm, D_SSM), lambda i: (i, gs_col // D_SSM))],
        out_specs=pl.BlockSpec((tm, D_SSM), lambda i: (i, 0)),
        out_shape=jax.ShapeDtypeStruct((m, D_SSM), BF16),
        compiler_params=_params("arbitrary"),
        name="glu",
    )(yg2, w_glu_b, b_glu3, zb2)


def _outproj_body(a_ref, b_ref, w1_ref, w2_ref, x_ref, g_ref, o_ref):
    acc = (jnp.dot(a_ref[...], w1_ref[...], preferred_element_type=F32)
           + jnp.dot(b_ref[...], w2_ref[...], preferred_element_type=F32))
    o_ref[...] = x_ref[...] + g_ref[...] * acc


def _out_proj(ys2, oa2, w_out_b4, x3, kx, mod4, kb, l, tm):
    m = x3.shape[1]
    tn = 1024
    r = mod4.shape[2]
    gate_blk = 2 * D_MODEL // tn
    return pl.pallas_call(
        _outproj_body,
        grid=(D_MODEL // tn, m // tm),
        in_specs=[pl.BlockSpec((tm, D_SSM), lambda j, i: (i, 0)),
                  pl.BlockSpec((tm, D_ATT), lambda j, i: (i, 0)),
                  pl.BlockSpec((None, None, D_SSM, tn), lambda j, i: (l, 0, 0, j)),
                  pl.BlockSpec((None, None, D_ATT, tn), lambda j, i: (l, 1, 0, j)),
                  pl.BlockSpec((None, tm, tn), lambda j, i: (kx, i, j)),
                  pl.BlockSpec((None, None, r, tn), lambda j, i: (l, kb, 0, gate_blk + j))],
        out_specs=pl.BlockSpec((tm, tn), lambda j, i: (i, j)),
        out_shape=jax.ShapeDtypeStruct((m, D_MODEL), F32),
        compiler_params=_params("arbitrary", "arbitrary"),
        name="out_proj",
    )(ys2, oa2, w_out_b4, w_out_b4, x3, mod4)


GO_TM = 512
GO_TN = 1024


def _mix_out_body(yg_ref, gs_ref, oa_ref, wg_ref, bg_ref, w1_ref, w2_ref, x_ref, gate_ref, xo_ref, ys_buf):
    @pl.when(pl.program_id(1) == 0)
    def _():
        _glu_body(yg_ref, wg_ref, bg_ref, gs_ref, ys_buf)

    acc = (jnp.dot(ys_buf[...], w1_ref[...], preferred_element_type=F32)
           + jnp.dot(oa_ref[...], w2_ref[...], preferred_element_type=F32))
    xo_ref[...] = x_ref[...] + gate_ref[...] * acc


def _mix_out(yg, z, oa, w_glu_b, b_glu3, w_out_b4, x3, kx, mod4, kb, l):
    seq = yg.shape[0]
    tm, tn = GO_TM, GO_TN
    gate_blk = 2 * D_MODEL // tn
    return pl.pallas_call(
        _mix_out_body,
        grid=(seq // tm, D_MODEL // tn),
        in_specs=[pl.BlockSpec((tm, D_SSM), lambda i, j: (i, 0)),
                  pl.BlockSpec((tm, D_SSM), lambda i, j: (i, Z_GS // D_SSM)),
                  pl.BlockSpec((tm, D_ATT), lambda i, j: (i, 0)),
                  pl.BlockSpec((None, D_SSM, D_SSM), lambda i, j: (l, 0, 0), pipeline_mode=pl.Buffered(1)),
                  pl.BlockSpec((None, 1, D_SSM), lambda i, j: (l, 0, 0)),
                  pl.BlockSpec((None, None, D_SSM, tn), lambda i, j: (l, 0, 0, j)),
                  pl.BlockSpec((None, None, D_ATT, tn), lambda i, j: (l, 1, 0, j)),
                  pl.BlockSpec((None, tm, tn), lambda i, j: (kx, i, j)),
                  pl.BlockSpec((None, None, 1, tn), lambda i, j: (l, kb, 0, gate_blk + j))],
        out_specs=pl.BlockSpec((tm, tn), lambda i, j: (i, j)),
        out_shape=jax.ShapeDtypeStruct((seq, D_MODEL), F32),
        scratch_shapes=[pltpu.VMEM((tm, D_SSM), BF16)],
        compiler_params=_params("arbitrary", "arbitrary"),
        name="mix_out",
    )(yg, z, oa, w_glu_b, b_glu3, w_out_b4, w_out_b4, x3, mod4)


def _final_norm_body(x_ref, g_ref, o_ref):
    x = x_ref[...]
    y = x * lax.rsqrt(jnp.mean(x * x, axis=-1, keepdims=True) + EPS)
    o_ref[...] = y * g_ref[...]


def _final_norm(x2, final_g2, tm):
    m = x2.shape[0]
    return pl.pallas_call(
        _final_norm_body,
        grid=(m // tm,),
        in_specs=[pl.BlockSpec((tm, D_MODEL), lambda i: (i, 0)),
                  pl.BlockSpec((1, D_MODEL), lambda i: (0, 0))],
        out_specs=pl.BlockSpec((tm, D_MODEL), lambda i: (i, 0)),
        out_shape=jax.ShapeDtypeStruct((m, D_MODEL), F32),
        compiler_params=_params("arbitrary"),
        name="final_norm",
    )(x2, final_g2)


def _final_norm_seqs_body(*refs):
    *x_refs, g_ref, o_ref = refs
    b = pl.program_id(0)
    x = x_refs[-1][...]
    for k in range(len(x_refs) - 2, -1, -1):
        x = jnp.where(b == k, x_refs[k][...], x)
    o_ref[...] = x * lax.rsqrt(jnp.mean(x * x, axis=-1, keepdims=True) + EPS) * g_ref[...]


def _final_norm_seqs(xs, final_g2, tm):
    seq = xs[0].shape[0]
    last = seq // tm - 1

    def xspec(k):
        return pl.BlockSpec((tm, D_MODEL), lambda b, i: (jnp.where(b == k, i, jnp.where(b > k, last, 0)), 0))

    return pl.pallas_call(
        _final_norm_seqs_body,
        grid=(len(xs), seq // tm),
        in_specs=[xspec(k) for k in range(len(xs))] + [pl.BlockSpec((1, D_MODEL), lambda b, i: (0, 0))],
        out_specs=pl.BlockSpec((None, tm, D_MODEL), lambda b, i: (b, i, 0)),
        out_shape=jax.ShapeDtypeStruct((len(xs), seq, D_MODEL), F32),
        compiler_params=_params("arbitrary", "arbitrary"),
        name="final_norm_seqs",
    )(*xs, final_g2)


def _rope_tables(pos):
    half = HEAD_DIM // 2
    inv = ROPE_THETA ** (-jnp.arange(half, dtype=F32) / half)
    ang = pos.astype(F32)[:, None] * inv[None, :]
    cos, sin = jnp.cos(ang), jnp.sin(ang)
    return jnp.concatenate([cos, cos], axis=-1), jnp.concatenate([-sin, sin], axis=-1)


def kernel(x_prompt, x_sample, c_prompt, c_sample, cache_win_k, cache_win_v, state_ssm_re, state_ssm_im, w_ada, b_ada, norm_g, w_in, ssm_a_re, ssm_a_im, ssm_log_dt, ssm_b_re, ssm_b_im, ssm_c_re, ssm_c_im, ssm_d, w_glu, b_glu, attn_sinks, w_out, final_g):
    bp, seq, _ = x_prompt.shape
    bs = x_sample.shape[0]
    past_len = 8192
    assert seq % (CHUNK * 128) == 0 and x_sample.shape[1] == 1 and cache_win_k.shape[2] == WINDOW
    assert bp + bs <= MOD_ROWS

    w_in_b = w_in.astype(BF16)
    w_glu_b = w_glu.astype(BF16)
    w_out_b4 = w_out.astype(BF16).reshape(DEPTH, 2, D_SSM, D_MODEL)
    norm_g3 = norm_g.reshape(DEPTH, 1, D_MODEL)
    b_glu3 = b_glu.reshape(DEPTH, 1, D_SSM)
    ssm_d3 = ssm_d.reshape(DEPTH, 1, D_SSM)

    c_all = jnp.concatenate([c_prompt, c_sample, jnp.zeros((MOD_ROWS - bp - bs, D_MODEL), F32)], axis=0)
    mod = _modulation(c_all, w_ada, b_ada)
    mod_p = mod[:, :bp].reshape(DEPTH, bp, 1, 3 * D_MODEL)
    mod_s = mod[:, bp:bp + bs].reshape(DEPTH, 1, bs, 3 * D_MODEL)

    prep = _ssm_prep(ssm_a_re, ssm_a_im, ssm_log_dt, ssm_b_re, ssm_b_im, ssm_c_re, ssm_c_im)
    cos_p, sin_p = _rope_tables(jnp.arange(seq))
    cos_s, sin_s = _rope_tables(jnp.full((bs,), past_len))

    xcur = [(x_prompt, k) for k in range(bp)]
    pk = [[None] * bp for _ in range(DEPTH)]
    pv = [[None] * bp for _ in range(DEPTH)]
    pr = [[None] * bp for _ in range(DEPTH)]
    pi = [[None] * bp for _ in range(DEPTH)]

    def finish(z, l, k, sa):
        yg, hr, hi, oa = sa
        x_new = _mix_out(yg, z, oa, w_glu_b, b_glu3, w_out_b4, xcur[k][0], xcur[k][1], mod_p, k, l)
        xcur[k] = (x_new[None], 0)
        pk[l][k] = z[seq - WINDOW:, Z_K:Z_K + D_KV].astype(F32).reshape(WINDOW, N_KV_HEADS, HEAD_DIM)
        pv[l][k] = z[seq - WINDOW:, Z_V:Z_V + D_KV].astype(F32).reshape(WINDOW, N_KV_HEADS, HEAD_DIM)
        pr[l][k] = hr.reshape(N_GROUPS, SSM_STATE)
        pi[l][k] = hi.reshape(N_GROUPS, SSM_STATE)

    pending = None
    for l in range(DEPTH):
        for k in range(bp):
            h = _rms_mod(xcur[k][0], xcur[k][1], norm_g3, mod_p, k, l, 256)
            if pending is None:
                z, = _prompt_stage(h, l, w_in_b, cos_p, sin_p, None, None, prep, ssm_d3, attn_sinks)
            else:
                z, *sa = _prompt_stage(h, l, w_in_b, cos_p, sin_p, pending[0], pending[1], prep, ssm_d3,
                                       attn_sinks)
                finish(*pending, sa)
            pending = (z, l, k)
    sa = _prompt_stage(None, None, w_in_b, cos_p, sin_p, pending[0], pending[1], prep, ssm_d3, attn_sinks)
    finish(*pending, sa)
    y_prompt = _final_norm_seqs([x[0] for x, _ in xcur], final_g.reshape(1, D_MODEL), 256)
    pk, pv, pr, pi = (jnp.stack([jnp.stack(row) for row in t]) for t in (pk, pv, pr, pi))

    xs = x_sample.reshape(bs, D_MODEL)
    sk, sv, sr, si = [], [], [], []
    for l in range(DEPTH):
        h = _rms_mod(xs[None], 0, norm_g3, mod_s, 0, l, bs)
        u, zb = _in_proj(h, w_in_b, cos_s, sin_s, l, bs)
        ut = u.reshape(bs, N_GROUPS, SSM_GROUP).transpose(1, 2, 0)
        ht_r = state_ssm_re[l].transpose(1, 2, 0)
        ht_i = state_ssm_im[l].transpose(1, 2, 0)
        ygt, nhr, nhi = _ssm_step(ut, ht_r, ht_i, prep, ssm_b_re, ssm_b_im, ssm_c_re, ssm_c_im, ssm_d, l)
        yg = ygt.transpose(2, 0, 1).reshape(bs, D_SSM).astype(BF16)
        zf = zb.astype(F32)
        q3 = zf[:, ZB_Q:ZB_Q + D_ATT].reshape(bs, N_HEADS, HEAD_DIM)
        ga3 = zf[:, ZB_GA:ZB_GA + D_ATT].reshape(bs, N_HEADS, HEAD_DIM)
        kn = zf[:, ZB_K:ZB_K + D_KV].reshape(bs, N_KV_HEADS, HEAD_DIM)
        vn = zf[:, ZB_V:ZB_V + D_KV].reshape(bs, N_KV_HEADS, HEAD_DIM)
        oa = _attention_step(q3, ga3, kn, vn, cache_win_k, cache_win_v, attn_sinks, l)
        ys = _glu(yg, zb, ZB_GS, w_glu_b, b_glu3, l, bs)
        xs = _out_proj(ys, oa.reshape(bs, D_ATT).astype(BF16), w_out_b4, xs[None], 0, mod_s, 0, l, bs)
        sk.append(kn.reshape(bs, 1, N_KV_HEADS, HEAD_DIM))
        sv.append(vn.reshape(bs, 1, N_KV_HEADS, HEAD_DIM))
        sr.append(nhr.transpose(2, 0, 1))
        si.append(nhi.transpose(2, 0, 1))
    y_sample = _final_norm(xs, final_g.reshape(1, D_MODEL), bs).reshape(bs, 1, D_MODEL)
    new_k, new_v = _roll_cache(cache_win_k, cache_win_v, jnp.stack(sk), jnp.stack(sv))

    return (y_prompt, y_sample, jnp.stack(pk), jnp.stack(pv), jnp.stack(pr), jnp.stack(pi),
            new_k, new_v, jnp.stack(sr), jnp.stack(si))
```

```python
import functools
import math

import jax
import jax.numpy as jnp
from jax import lax
from jax.experimental import pallas as pl
from jax.experimental.pallas import tpu as pltpu

F32 = jnp.float32
BF16 = jnp.bfloat16

D_MODEL = 4096
DEPTH = 4
D_SSM = 2048
D_ATT = 2048
SSM_GROUP = 16
N_GROUPS = 128
SSM_STATE = 64
HEAD_DIM = 128
N_HEADS = 16
N_KV_HEADS = 4
KV_GROUP = 4
D_KV = 512
WINDOW = 128
ROPE_THETA = 10000.0
EPS = 1e-6
BLOCK = 128
D_IN = 9216
ATT_SCALE = HEAD_DIM ** -0.5

IN_TN = 512
IN_TILES = D_IN // IN_TN
U_TILES = D_SSM // IN_TN
ZB_COLS = D_IN - D_SSM
ZB_Q, ZB_GA, ZB_GS, ZB_K, ZB_V = 0, 2048, 4096, 6144, 6656

CHUNK = 16
SSM_GB = 8
SSM_W = SSM_GB * SSM_GROUP
SCAN_LANES = 128
MOD_ROWS = 40

VMEM_LIMIT_BYTES = 56 * 1024 * 1024


def _params(*sem, vmem_limit_bytes=VMEM_LIMIT_BYTES):
    return pltpu.CompilerParams(dimension_semantics=sem, vmem_limit_bytes=vmem_limit_bytes)


def _mod_body(c_ref, w_ref, b_ref, o_ref):
    a = jax.nn.silu(c_ref[...]).astype(BF16)
    o_ref[...] = jnp.dot(a, w_ref[...].astype(BF16), preferred_element_type=F32) + b_ref[...]


def _modulation(c_all, w_ada, b_ada):
    tn = 512
    return pl.pallas_call(
        _mod_body,
        grid=(DEPTH, 3 * D_MODEL // tn),
        in_specs=[pl.BlockSpec((MOD_ROWS, D_MODEL), lambda l, j: (0, 0)),
                  pl.BlockSpec((None, D_MODEL, tn), lambda l, j: (l, 0, j)),
                  pl.BlockSpec((None, 1, tn), lambda l, j: (l, 0, j))],
        out_specs=pl.BlockSpec((None, MOD_ROWS, tn), lambda l, j: (l, 0, j)),
        out_shape=jax.ShapeDtypeStruct((DEPTH, MOD_ROWS, 3 * D_MODEL), F32),
        compiler_params=_params("arbitrary", "arbitrary"),
        name="adaln_mod",
    )(c_all, w_ada, b_ada.reshape(DEPTH, 1, 3 * D_MODEL))


def _rmsmod_body(x_ref, g_ref, sc_ref, sh_ref, o_ref):
    x = x_ref[...]
    y = x * lax.rsqrt(jnp.mean(x * x, axis=-1, keepdims=True) + EPS)
    o_ref[...] = ((y * g_ref[...]) * (1.0 + sc_ref[...]) + sh_ref[...]).astype(BF16)


def _rms_mod(x3, kx, norm_g3, mod4, kb, l, tm):
    lr = x3.shape[1]
    r = mod4.shape[2]
    return pl.pallas_call(
        _rmsmod_body,
        grid=(lr // tm,),
        in_specs=[pl.BlockSpec((None, tm, D_MODEL), lambda i: (kx, i, 0)),
                  pl.BlockSpec((None, 1, D_MODEL), lambda i: (l, 0, 0)),
                  pl.BlockSpec((None, None, r, D_MODEL), lambda i: (l, kb, 0, 1)),
                  pl.BlockSpec((None, None, r, D_MODEL), lambda i: (l, kb, 0, 0))],
        out_specs=pl.BlockSpec((tm, D_MODEL), lambda i: (i, 0)),
        out_shape=jax.ShapeDtypeStruct((lr, D_MODEL), BF16),
        compiler_params=_params("arbitrary"),
        name="rms_mod",
    )(x3, norm_g3, mod4, mod4)


def _w_in_tile(j):
    return jnp.where(j < 4, j, jnp.where(j < 8, j + 4, jnp.where(j < 12, j + 6, jnp.where(j < 16, j - 8, j - 4))))


def _rope(x, cos, sin):
    return x * cos + pltpu.roll(x, HEAD_DIM // 2, 1) * sin


def _inproj_body(h_ref, w_ref, cos_ref, sin_ref, u_ref, zb_ref):
    j = pl.program_id(1)
    half = IN_TN // 2

    def halves(epilogue):
        for c in range(2):
            sl = slice(c * half, (c + 1) * half)
            epilogue(sl, jnp.dot(h_ref[...], w_ref[:, sl], preferred_element_type=F32))

    @pl.when(j < 4)
    def _():
        def plain(sl, acc):
            u_ref[:, sl] = acc
        halves(plain)

    @pl.when(((j >= 4) & (j < 8)) | (j == 16))
    def _():
        def rope(sl, acc):
            for hh in range(half // HEAD_DIM):
                hs = slice(hh * HEAD_DIM, (hh + 1) * HEAD_DIM)
                zb_ref[:, sl.start + hs.start:sl.start + hs.stop] = _rope(
                    acc[:, hs], cos_ref[...], sin_ref[...]).astype(BF16)
        halves(rope)

    @pl.when((j >= 8) & (j < 16))
    def _():
        def silu(sl, acc):
            zb_ref[:, sl] = jax.nn.silu(acc).astype(BF16)
        halves(silu)

    @pl.when(j == 17)
    def _():
        def cast(sl, acc):
            zb_ref[:, sl] = acc.astype(BF16)
        halves(cast)


def _in_proj(h2, w_in_b, cos_t, sin_t, l, tm):
    m = h2.shape[0]
    nt = cos_t.shape[0] // tm
    return pl.pallas_call(
        _inproj_body,
        grid=(m // tm, IN_TILES),
        in_specs=[pl.BlockSpec((tm, D_MODEL), lambda i, j: (i, 0)),
                  pl.BlockSpec((None, D_MODEL, IN_TN), lambda i, j: (l, 0, _w_in_tile(j))),
                  pl.BlockSpec((tm, HEAD_DIM), lambda i, j: (i % nt, 0)),
                  pl.BlockSpec((tm, HEAD_DIM), lambda i, j: (i % nt, 0))],
        out_specs=[pl.BlockSpec((tm, IN_TN), lambda i, j: (i, jnp.minimum(j, U_TILES - 1))),
                   pl.BlockSpec((tm, IN_TN), lambda i, j: (i, jnp.maximum(j - U_TILES, 0)))],
        out_shape=[jax.ShapeDtypeStruct((m, D_SSM), F32),
                   jax.ShapeDtypeStruct((m, ZB_COLS), BF16)],
        compiler_params=_params("arbitrary", "arbitrary"),
        name="in_proj",
    )(h2, w_in_b, cos_t, sin_t)


def _cmul(ar, ai, br, bi):
    return ar * br - ai * bi, ar * bi + ai * br


def _discretise(ar, ai, log_dt):
    dt = jnp.exp(log_dt)
    mag = jnp.exp(ar * dt)
    lr = mag * jnp.cos(ai * dt)
    li = mag * jnp.sin(ai * dt)
    nr = lr - 1.0
    ni = li
    den = ar * ar + ai * ai
    qr = (nr * ar + ni * ai) / den
    qi = (ni * ar - nr * ai) / den
    return lr, li, qr, qi


def _powers(lr, li, n):
    pr, pi = [jnp.ones_like(lr)], [jnp.zeros_like(li)]
    for _ in range(n):
        r, i = _cmul(pr[-1], pi[-1], lr, li)
        pr.append(r)
        pi.append(i)
    return pr, pi


PREP_GB = 16


def _split_bf16(x):
    hi = x.astype(BF16)
    return hi, (x - hi.astype(F32)).astype(BF16)


def _ssm_prep_body(a2r_ref, a2i_ref, ldt_ref, cc_ref, cs_ref, bc_ref, bs_ref,
                   mt_ref, ws_ref, wc_ref, l16r_ref, l16i_ref, l1r_ref, l1i_ref, qr_ref, qi_ref):
    gb, t_len, p = PREP_GB, CHUNK, SSM_STATE
    kdim = t_len * SSM_GROUP
    lr, li, qr, qi = _discretise(a2r_ref[...], a2i_ref[...], ldt_ref[...])
    sgn = jnp.where(lax.broadcasted_iota(jnp.int32, (1, 2 * p), 1) < p, 1.0, -1.0)
    pr, pi = _powers(lr, li, t_len)
    mag2 = lr * lr + li * li
    ir, ii = _powers(lr / mag2, -li / mag2, t_len - 1)
    prs = [x * sgn for x in pr]
    w_inv = [_cmul(qr, qi, ir[s], ii[s]) for s in range(t_len)]
    w_15 = [_cmul(qr, qi, pr[t_len - 1 - s], pi[t_len - 1 - s]) for s in range(t_len)]
    w_inv = [(wr, wi * sgn) for wr, wi in w_inv]
    w_15 = [(wr, wi * sgn) for wr, wi in w_15]

    row_t = lax.broadcasted_iota(jnp.int32, (kdim, kdim), 0) >> 4
    col_s = lax.broadcasted_iota(jnp.int32, (kdim, kdim), 1) >> 4
    causal = row_t >= col_s
    nt = (((1,), (1,)), ((), ()))
    for g in range(gb):
        row = slice(g, g + 1)
        cc, cs = cc_ref[g], cs_ref[g]
        bc, bs = bc_ref[g], bs_ref[g]
        cl = jnp.concatenate([cc * prs[t][row] - cs * pi[t][row] for t in range(t_len)], axis=0)
        wc = jnp.concatenate([cc * prs[t + 1][row] - cs * pi[t + 1][row] for t in range(t_len)], axis=0)
        bl = jnp.concatenate([bc * w_inv[s][0][row] - bs * w_inv[s][1][row] for s in range(t_len)], axis=0)
        ws = jnp.concatenate([bc * w_15[s][0][row] - bs * w_15[s][1][row] for s in range(t_len)], axis=0)
        a_hi, a_lo = _split_bf16(cl)
        b_hi, b_lo = _split_bf16(bl)
        m = (lax.dot_general(a_hi, b_hi, nt, preferred_element_type=F32)
             + lax.dot_general(a_hi, b_lo, nt, preferred_element_type=F32)
             + lax.dot_general(a_lo, b_hi, nt, preferred_element_type=F32))
        mt_ref[g] = jnp.where(causal, m, 0.0).astype(BF16)
        ws_ref[g] = ws.T.astype(BF16)
        wc_ref[g] = wc.astype(BF16)

    pad = jnp.zeros((2 * p - gb, 2 * p), F32)
    outs = ((pr[t_len], l16r_ref), (pi[t_len], l16i_ref), (lr, l1r_ref), (li, l1i_ref),
            (qr, qr_ref), (qi, qi_ref))
    for val, ref in outs:
        col = jnp.concatenate([val, pad], axis=0).T
        for g in range(gb):
            ref[g] = jnp.broadcast_to(col[:p, g:g + 1], ref.shape[1:])


def _ssm_prep(a_re, a_im, log_dt, b_re, b_im, c_re, c_im):
    g, p, kdim, gb = N_GROUPS, SSM_STATE, CHUNK * SSM_GROUP, PREP_GB
    a2r = jnp.concatenate([a_re, a_re], axis=-1)
    a2i = jnp.concatenate([a_im, a_im], axis=-1)
    ldt = log_dt.reshape(DEPTH, g, 1)
    bt_re = jnp.swapaxes(b_re, -1, -2)
    bt_im = jnp.swapaxes(b_im, -1, -2)
    cc = jnp.concatenate([c_re, c_im], axis=-1)
    cs = jnp.concatenate([c_im, c_re], axis=-1)
    bc = jnp.concatenate([bt_re, bt_im], axis=-1)
    bs = jnp.concatenate([bt_im, bt_re], axis=-1)

    def spec(a, b):
        return pl.BlockSpec((None, gb, a, b), lambda l, i: (l, i, 0, 0))

    def spec2(b):
        return pl.BlockSpec((None, gb, b), lambda l, i: (l, i, 0))

    col_shape = jax.ShapeDtypeStruct((DEPTH, g, p, 1), F32)
    wide_shape = jax.ShapeDtypeStruct((DEPTH, g, p, SCAN_LANES), F32)
    return pl.pallas_call(
        _ssm_prep_body,
        grid=(DEPTH, g // gb),
        in_specs=[spec2(2 * p), spec2(2 * p), spec2(1),
                  spec(SSM_GROUP, 2 * p), spec(SSM_GROUP, 2 * p), spec(SSM_GROUP, 2 * p),
                  spec(SSM_GROUP, 2 * p)],
        out_specs=([spec(kdim, kdim), spec(2 * p, kdim), spec(kdim, 2 * p)]
                   + [spec(p, SCAN_LANES)] * 2 + [spec(p, 1)] * 4),
        out_shape=[jax.ShapeDtypeStruct((DEPTH, g, kdim, kdim), BF16),
                   jax.ShapeDtypeStruct((DEPTH, g, 2 * p, kdim), BF16),
                   jax.ShapeDtypeStruct((DEPTH, g, kdim, 2 * p), BF16)] + [wide_shape] * 2 + [col_shape] * 4,
        compiler_params=_params("arbitrary", "arbitrary"),
        name="ssm_prep",
    )(a2r, a2i, ldt, cc, cs, bc, bs)


def _ssm_part(u_ref, mt_ref, ws_ref, wc_ref, l16r_ref, l16i_ref, d_ref,
              yg_ref, hr_ref, hi_ref, ubuf, ybuf):
    gb, t_len, p = SSM_GB, CHUNK, SSM_STATE
    r_len = u_ref.shape[0] // t_len
    ubuf[...] = u_ref[...].astype(F32)
    slab_t = [ubuf[pl.ds(s, r_len, stride=t_len), :].T.astype(BF16) for s in range(t_len)]

    yl, xr, xi = [], [], []
    for g in range(gb):
        rows = slice(g * SSM_GROUP, (g + 1) * SSM_GROUP)
        s_g = jnp.concatenate([st[rows, :] for st in slab_t], axis=0)
        both = jnp.dot(jnp.concatenate([mt_ref[g], ws_ref[g]], axis=0), s_g, preferred_element_type=F32)
        kd = t_len * SSM_GROUP
        yl.append(both[:kd, :])
        xr.append(both[kd:kd + p, :])
        xi.append(both[kd + p:, :])
    h_r = jnp.concatenate(xr, axis=0)
    h_i = jnp.concatenate(xi, axis=0)
    yield

    a_r = l16r_ref[...].reshape(gb * p, r_len)
    a_i = l16i_ref[...].reshape(gb * p, r_len)
    lane = lax.broadcasted_iota(jnp.int32, (gb * p, r_len), 1)
    k = 1
    while k < r_len:
        keep = lane >= k
        s_r = jnp.where(keep, pltpu.roll(h_r, k, 1), 0.0)
        s_i = jnp.where(keep, pltpu.roll(h_i, k, 1), 0.0)
        h_r, h_i = h_r + a_r * s_r - a_i * s_i, h_i + a_r * s_i + a_i * s_r
        a_r, a_i = a_r * a_r - a_i * a_i, 2.0 * a_r * a_i
        k *= 2
    hr_ref[...] = h_r[:, r_len - 1:r_len].reshape(gb, p, 1)
    hi_ref[...] = h_i[:, r_len - 1:r_len].reshape(gb, p, 1)

    first = lane >= 1
    c_r = jnp.where(first, pltpu.roll(h_r, 1, 1), 0.0).astype(BF16)
    c_i = jnp.where(first, pltpu.roll(h_i, 1, 1), 0.0).astype(BF16)
    ys = []
    for g in range(gb):
        rows = slice(g * p, (g + 1) * p)
        c_g = jnp.concatenate([c_r[rows, :], c_i[rows, :]], axis=0)
        ys.append(yl[g] + jnp.dot(wc_ref[g], c_g, preferred_element_type=F32))
    yield
    d = d_ref[...]
    for t in range(t_len):
        rows = slice(t * SSM_GROUP, (t + 1) * SSM_GROUP)
        y_t = jnp.concatenate([y[rows, :] for y in ys], axis=0)
        u_t = ubuf[pl.ds(t, r_len, stride=t_len), :]
        ybuf[pl.ds(t, r_len, stride=t_len), :] = jax.nn.gelu(y_t.T + d * u_t)
    yg_ref[...] = ybuf[...].astype(BF16)


SSM_STEP_GB = 16


def _ssm_step_body(lr_ref, li_ref, qr_ref, qi_ref, br_ref, bi_ref, cr_ref, ci_ref, d_ref, ut_ref,
                   hr_ref, hi_ref, yt_ref, nhr_ref, nhi_ref):
    lr, li = lr_ref[...], li_ref[...]
    bbr, bbi = _cmul(qr_ref[...], qi_ref[...], br_ref[...], bi_ref[...])
    for g in range(SSM_STEP_GB):
        u = ut_ref[g]
        ub = u.astype(BF16)
        x_r = jnp.dot(bbr[g].astype(BF16), ub, preferred_element_type=F32)
        x_i = jnp.dot(bbi[g].astype(BF16), ub, preferred_element_type=F32)
        h_r, h_i = hr_ref[g], hi_ref[g]
        n_r = lr[g] * h_r - li[g] * h_i + x_r
        n_i = lr[g] * h_i + li[g] * h_r + x_i
        y = (jnp.dot(cr_ref[g].astype(BF16), n_r.astype(BF16), preferred_element_type=F32)
             - jnp.dot(ci_ref[g].astype(BF16), n_i.astype(BF16), preferred_element_type=F32))
        yt_ref[g] = jax.nn.gelu(y + d_ref[g] * u)
        nhr_ref[g] = n_r
        nhi_ref[g] = n_i


def _ssm_step(ut, ht_r, ht_i, prep, b_re, b_im, c_re, c_im, ssm_d, l):
    l1r, l1i, qr, qi = prep[5:9]
    g, p, gb = N_GROUPS, SSM_STATE, SSM_STEP_GB
    bs = ut.shape[-1]

    def lspec(a, c):
        return pl.BlockSpec((None, gb, a, c), lambda i: (l, i, 0, 0))

    def aspec(a):
        return pl.BlockSpec((gb, a, bs), lambda i: (i, 0, 0))

    return pl.pallas_call(
        _ssm_step_body,
        grid=(g // gb,),
        in_specs=[lspec(p, 1), lspec(p, 1), lspec(p, 1), lspec(p, 1), lspec(p, SSM_GROUP), lspec(p, SSM_GROUP),
                  lspec(SSM_GROUP, p), lspec(SSM_GROUP, p), lspec(SSM_GROUP, 1),
                  aspec(SSM_GROUP), aspec(p), aspec(p)],
        out_specs=[aspec(SSM_GROUP), aspec(p), aspec(p)],
        out_shape=[jax.ShapeDtypeStruct((g, SSM_GROUP, bs), F32),
                   jax.ShapeDtypeStruct((g, p, bs), F32),
                   jax.ShapeDtypeStruct((g, p, bs), F32)],
        compiler_params=_params("arbitrary"),
        name="ssm_step",
    )(l1r, l1i, qr, qi, b_re, b_im, c_re, c_im, ssm_d.reshape(DEPTH, g, SSM_GROUP, 1), ut, ht_r, ht_i)


def _attn_part(l, n, q_ref, ga_ref, kc_ref, kp_ref, vc_ref, vp_ref, sink_ref, o_ref):
    cols = KV_GROUP * BLOCK
    j = lax.broadcasted_iota(jnp.int32, (2 * BLOCK, cols), 0)
    r = lax.broadcasted_iota(jnp.int32, (2 * BLOCK, cols), 1) & (BLOCK - 1)
    no_prev = jnp.where(n > 0, 0, 2 * BLOCK)
    valid = ((j < BLOCK) & (j > r + no_prev)) | ((j >= BLOCK) & (j - BLOCK <= r))
    ones = jnp.ones((HEAD_DIM, 2 * BLOCK), BF16)
    nt = (((1,), (1,)), ((), ()))
    heads = [[kh * KV_GROUP + g for g in range(KV_GROUP)] for kh in range(N_KV_HEADS)]
    scores = []
    for kh in range(N_KV_HEADS):
        ksl = slice(kh * HEAD_DIM, (kh + 1) * HEAD_DIM)
        k = jnp.concatenate([kp_ref[:, ksl], kc_ref[:, ksl]], axis=0)
        q4 = jnp.concatenate([q_ref[:, h * HEAD_DIM:(h + 1) * HEAD_DIM] for h in heads[kh]], axis=0)
        scores.append(lax.dot_general(k, q4, nt, preferred_element_type=F32))
    yield
    outs = []
    for kh in range(N_KV_HEADS):
        ksl = slice(kh * HEAD_DIM, (kh + 1) * HEAD_DIM)
        vt = jnp.concatenate([vp_ref[:, ksl].T, vc_ref[:, ksl].T], axis=1)
        v1t = jnp.concatenate([vt, ones], axis=0)
        st = jnp.where(valid, scores[kh] * ATT_SCALE, -jnp.inf)
        sink = jnp.concatenate([jnp.full((1, BLOCK), sink_ref[l, h], F32) for h in heads[kh]], axis=1)
        m = jnp.maximum(jnp.max(st, axis=0, keepdims=True), sink)
        e = jnp.exp(st - m).astype(BF16)
        o1 = jnp.dot(v1t, e, preferred_element_type=F32)
        outs.append((o1, jnp.exp(sink - m)))
    yield
    for kh in range(N_KV_HEADS):
        o1, e_sink = outs[kh]
        ot = o1[:HEAD_DIM, :] / (o1[HEAD_DIM:, :] + e_sink)
        for g, h in enumerate(heads[kh]):
            hsl = slice(h * HEAD_DIM, (h + 1) * HEAD_DIM)
            o_ref[:, hsl] = (ot[:, g * BLOCK:(g + 1) * BLOCK].T * ga_ref[:, hsl].astype(F32)).astype(BF16)


Z_Q, Z_GA, Z_GS, Z_K, Z_V = 2048, 4096, 6144, 8192, 8704
SA_STEPS = N_GROUPS // SSM_GB
IN_ROWS = 512


def _in_part(j, h_ref, w_ref, cos_ref, sin_ref, z_ref):
    is_rope = ((j >= 4) & (j < 8)) | (j == 16)
    is_silu = (j >= 8) & (j < 16)
    for rc in range(h_ref.shape[0] // IN_ROWS):
        rs = slice(rc * IN_ROWS, (rc + 1) * IN_ROWS)
        acc = jnp.dot(h_ref[rs, :], w_ref[...], preferred_element_type=F32)
        cos, sin = cos_ref[rs, :], sin_ref[rs, :]
        roped = jnp.concatenate([_rope(acc[:, hh * HEAD_DIM:(hh + 1) * HEAD_DIM], cos, sin)
                                 for hh in range(IN_TN // HEAD_DIM)], axis=1)
        out = jnp.where(is_rope, roped, jnp.where(is_silu, jax.nn.silu(acc), acc))
        z_ref[rs, :] = out.astype(BF16)
        yield


def _stage_body(has_in, has_sa, l_prev, *refs):
    j = pl.program_id(0)
    refs = list(refs)
    in_refs = [refs.pop(0) for _ in range(4)] if has_in else []
    sa_refs = [refs.pop(0) for _ in range(14)] if has_sa else []
    proj = _in_part(j, *in_refs, refs.pop(0)) if has_in else iter(())
    side = []
    if has_sa:
        u_ref, q_ref, ga_ref, kc_ref, kp_ref, vc_ref, vp_ref = sa_refs[:7]
        mt_ref, ws_ref, wc_ref, l16r_ref, l16i_ref, d_ref, sink_ref = sa_refs[7:]
        yg_ref, hr_ref, hi_ref, oa_ref, ubuf, ybuf = refs
        side = [_ssm_part(u_ref, mt_ref, ws_ref, wc_ref, l16r_ref, l16i_ref, d_ref, yg_ref, hr_ref,
                          hi_ref, ubuf, ybuf),
                _attn_part(l_prev, jnp.minimum(j, SA_STEPS - 1), q_ref, ga_ref, kc_ref, kp_ref, vc_ref,
                           vp_ref, sink_ref, oa_ref)]
    live = True
    while live:
        live = False
        for gen in side + [proj]:
            if next(gen, "done") != "done":
                live = True


def _prompt_stage(h, l_in, w_in_b, cos_t, sin_t, z_prev, l_prev, prep, ssm_d3, sinks):
    has_in, has_sa = h is not None, z_prev is not None
    seq = (h if has_in else z_prev).shape[0]
    gb, p, kdim = SSM_GB, SSM_STATE, CHUNK * SSM_GROUP
    in_specs, args, out_specs, out_shape, scratch = [], [], [], [], []

    def jm(j):
        return jnp.minimum(j, SA_STEPS - 1)

    if has_in:
        table = pl.BlockSpec((seq, HEAD_DIM), lambda j: (0, 0))
        in_specs += [pl.BlockSpec((seq, D_MODEL), lambda j: (0, 0), pipeline_mode=pl.Buffered(1)),
                     pl.BlockSpec((None, D_MODEL, IN_TN), lambda j: (l_in, 0, _w_in_tile(j))), table, table]
        args += [h, w_in_b, cos_t, sin_t]
        out_specs.append(pl.BlockSpec((seq, IN_TN), lambda j: (0, j)))
        out_shape.append(jax.ShapeDtypeStruct((seq, D_IN), BF16))
    if has_sa:
        def wspec(a, c):
            return pl.BlockSpec((None, gb, a, c), lambda j: (l_prev, jm(j), 0, 0))

        def kv(col, prev):
            if prev:
                return pl.BlockSpec((BLOCK, D_KV), lambda j: (jnp.maximum(jm(j) - 1, 0), col // D_KV))
            return pl.BlockSpec((BLOCK, D_KV), lambda j: (jm(j), col // D_KV))

        st_spec = pl.BlockSpec((gb, p, 1), lambda j: (jm(j), 0, 0))
        in_specs += [pl.BlockSpec((seq, SSM_W), lambda j: (0, jm(j))),
                     pl.BlockSpec((BLOCK, D_ATT), lambda j: (jm(j), Z_Q // D_ATT)),
                     pl.BlockSpec((BLOCK, D_ATT), lambda j: (jm(j), Z_GA // D_ATT)),
                     kv(Z_K, False), kv(Z_K, True), kv(Z_V, False), kv(Z_V, True),
                     wspec(kdim, kdim), wspec(2 * p, kdim), wspec(kdim, 2 * p),
                     wspec(p, SCAN_LANES), wspec(p, SCAN_LANES),
                     pl.BlockSpec((None, 1, SSM_W), lambda j: (l_prev, 0, jm(j))),
                     pl.BlockSpec(memory_space=pltpu.SMEM)]
        args += [z_prev] * 7 + list(prep[:5]) + [ssm_d3, sinks]
        out_specs += [pl.BlockSpec((seq, SSM_W), lambda j: (0, jm(j))), st_spec, st_spec,
                      pl.BlockSpec((BLOCK, D_ATT), lambda j: (jm(j), 0))]
        out_shape += [jax.ShapeDtypeStruct((seq, D_SSM), BF16),
                      jax.ShapeDtypeStruct((N_GROUPS, p, 1), F32),
                      jax.ShapeDtypeStruct((N_GROUPS, p, 1), F32),
                      jax.ShapeDtypeStruct((seq, D_ATT), BF16)]
        scratch = [pltpu.VMEM((seq, SSM_W), F32), pltpu.VMEM((seq, SSM_W), F32)]
    return pl.pallas_call(
        functools.partial(_stage_body, has_in, has_sa, l_prev),
        grid=(IN_TILES if has_in else SA_STEPS,),
        in_specs=in_specs, out_specs=out_specs, out_shape=out_shape, scratch_shapes=scratch,
        compiler_params=_params("arbitrary"),
        name="stage" + ("_in" if has_in else "") + ("_sa" if has_sa else ""),
    )(*args)


STEP_SB = 8


def _attn_step_body(l, q_ref, ga_ref, kn_ref, vn_ref, ck_ref, cv_ref, sink_ref, o_ref):
    rows = WINDOW * N_KV_HEADS
    head = lax.broadcasted_iota(jnp.int32, (N_HEADS, rows), 0)
    col = lax.broadcasted_iota(jnp.int32, (N_HEADS, rows), 1)
    valid = ((col & (N_KV_HEADS - 1)) == (head >> 2)) & (col >= N_KV_HEADS)
    sink = jnp.concatenate([jnp.full((1, 1), sink_ref[l, h], F32) for h in range(N_HEADS)], axis=0)
    nt = (((1,), (1,)), ((), ()))

    def per_head(x4):
        return jnp.concatenate([jnp.broadcast_to(x4[kh:kh + 1, :], (KV_GROUP, HEAD_DIM))
                                for kh in range(N_KV_HEADS)], axis=0)

    for b in range(STEP_SB):
        q = q_ref[b].astype(BF16)
        kn = per_head(kn_ref[b]).astype(BF16).astype(F32)
        vn = per_head(vn_ref[b]).astype(BF16).astype(F32)
        s = lax.dot_general(q, ck_ref[b].astype(BF16), nt, preferred_element_type=F32) * ATT_SCALE
        s = jnp.where(valid, s, -jnp.inf)
        s_new = jnp.sum(q.astype(F32) * kn, axis=-1, keepdims=True) * ATT_SCALE
        m = jnp.maximum(jnp.maximum(jnp.max(s, axis=-1, keepdims=True), s_new), sink)
        e = jnp.exp(s - m)
        e_new = jnp.exp(s_new - m)
        den = jnp.sum(e, axis=-1, keepdims=True) + e_new + jnp.exp(sink - m)
        o = (jnp.dot((e / den).astype(BF16), cv_ref[b].astype(BF16), preferred_element_type=F32)
             + (e_new / den).astype(BF16).astype(F32) * vn)
        o_ref[b] = o * ga_ref[b]


def _attention_step(q3, ga3, kn3, vn3, cache_k, cache_v, sinks, l):
    bs = q3.shape[0]
    sb = STEP_SB
    rows = WINDOW * N_KV_HEADS
    hspec = pl.BlockSpec((sb, N_HEADS, HEAD_DIM), lambda i: (i, 0, 0))
    nspec = pl.BlockSpec((sb, N_KV_HEADS, HEAD_DIM), lambda i: (i, 0, 0))
    cspec = pl.BlockSpec((None, sb, rows, HEAD_DIM), lambda i: (l, i, 0, 0))
    ck = cache_k.reshape(DEPTH, bs, rows, HEAD_DIM)
    cv = cache_v.reshape(DEPTH, bs, rows, HEAD_DIM)
    return pl.pallas_call(
        functools.partial(_attn_step_body, l),
        grid=(bs // sb,),
        in_specs=[hspec, hspec, nspec, nspec, cspec, cspec, pl.BlockSpec(memory_space=pltpu.SMEM)],
        out_specs=hspec,
        out_shape=jax.ShapeDtypeStruct((bs, N_HEADS, HEAD_DIM), F32),
        compiler_params=_params("arbitrary"),
        name="swa_step",
    )(q3, ga3, kn3, vn3, ck, cv, sinks)


def _roll_cache_body(ck_ref, cv_ref, kn_ref, vn_ref, ok_ref, ov_ref):
    for c_ref, n_ref, o_ref in ((ck_ref, kn_ref, ok_ref), (cv_ref, vn_ref, ov_ref)):
        o_ref[:, :WINDOW - 1] = c_ref[:, 1:]
        o_ref[:, WINDOW - 1:] = n_ref[...]


def _roll_cache(cache_k, cache_v, k_new, v_new):
    bs = cache_k.shape[1]
    sb = STEP_SB
    cspec = pl.BlockSpec((None, sb, WINDOW, N_KV_HEADS, HEAD_DIM), lambda l, i: (l, i, 0, 0, 0))
    nspec = pl.BlockSpec((None, sb, 1, N_KV_HEADS, HEAD_DIM), lambda l, i: (l, i, 0, 0, 0))
    shape = jax.ShapeDtypeStruct(cache_k.shape, F32)
    return pl.pallas_call(
        _roll_cache_body,
        grid=(DEPTH, bs // sb),
        in_specs=[cspec, cspec, nspec, nspec],
        out_specs=[cspec, cspec],
        out_shape=[shape, shape],
        compiler_params=_params("arbitrary", "arbitrary"),
        name="roll_cache",
    )(cache_k, cache_v, k_new, v_new)


def _glu_body(y_ref, w_ref, b_ref, gs_ref, o_ref):
    y = y_ref[...]
    step = 512
    for c in range(D_SSM // step):
        sl = slice(c * step, (c + 1) * step)
        acc = jnp.dot(y, w_ref[:, sl], preferred_element_type=F32) + b_ref[:, sl]
        o_ref[:, sl] = (y_ref[:, sl].astype(F32) * jax.nn.sigmoid(acc)
                        * gs_ref[:, sl].astype(F32)).astype(BF16)


def _glu(yg2, zb2, gs_col, w_glu_b, b_glu3, l, tm):
    m = yg2.shape[0]
    return pl.pallas_call(
        _glu_body,
        grid=(m // tm,),
        in_specs=[pl.BlockSpec((tm, D_SSM), lambda i: (i, 0)),
                  pl.BlockSpec((None, D_SSM, D_SSM), lambda i: (l, 0, 0)),
                  pl.BlockSpec((None, 1, D_SSM), lambda i: (l, 0, 0)),
                  pl.BlockSpec((tm, D_SSM), lambda i: (i, gs_col // D_SSM))],
        out_specs=pl.BlockSpec((tm, D_SSM), lambda i: (i, 0)),
        out_shape=jax.ShapeDtypeStruct((m, D_SSM), BF16),
        compiler_params=_params("arbitrary"),
        name="glu",
    )(yg2, w_glu_b, b_glu3, zb2)


def _outproj_body(a_ref, b_ref, w1_ref, w2_ref, x_ref, g_ref, o_ref):
    acc = (jnp.dot(a_ref[...], w1_ref[...], preferred_element_type=F32)
           + jnp.dot(b_ref[...], w2_ref[...], preferred_element_type=F32))
    o_ref[...] = x_ref[...] + g_ref[...] * acc


def _out_proj(ys2, oa2, w_out_b4, x3, kx, mod4, kb, l, tm):
    m = x3.shape[1]
    tn = 1024
    r = mod4.shape[2]
    gate_blk = 2 * D_MODEL // tn
    return pl.pallas_call(
        _outproj_body,
        grid=(D_MODEL // tn, m // tm),
        in_specs=[pl.BlockSpec((tm, D_SSM), lambda j, i: (i, 0)),
                  pl.BlockSpec((tm, D_ATT), lambda j, i: (i, 0)),
                  pl.BlockSpec((None, None, D_SSM, tn), lambda j, i: (l, 0, 0, j)),
                  pl.BlockSpec((None, None, D_ATT, tn), lambda j, i: (l, 1, 0, j)),
                  pl.BlockSpec((None, tm, tn), lambda j, i: (kx, i, j)),
                  pl.BlockSpec((None, None, r, tn), lambda j, i: (l, kb, 0, gate_blk + j))],
        out_specs=pl.BlockSpec((tm, tn), lambda j, i: (i, j)),
        out_shape=jax.ShapeDtypeStruct((m, D_MODEL), F32),
        compiler_params=_params("arbitrary", "arbitrary"),
        name="out_proj",
    )(ys2, oa2, w_out_b4, w_out_b4, x3, mod4)


GO_TM = 256
GO_TN = 1024


def _mix_out_body(yg_ref, gs_ref, oa_ref, wg_ref, bg_ref, w1_ref, w2_ref, x_ref, gate_ref, xo_ref, ys_buf):
    i = pl.program_id(1)

    @pl.when(pl.program_id(0) == 0)
    def _():
        _glu_body(yg_ref, wg_ref, bg_ref, gs_ref, ys_buf.at[i])

    acc = (jnp.dot(ys_buf[i], w1_ref[...], preferred_element_type=F32)
           + jnp.dot(oa_ref[...], w2_ref[...], preferred_element_type=F32))
    xo_ref[...] = x_ref[...] + gate_ref[...] * acc


def _mix_out(yg, z, oa, w_glu_b, b_glu3, w_out_b4, x3, kx, mod4, kb, l):
    seq = yg.shape[0]
    tm, tn = GO_TM, GO_TN
    gate_blk = 2 * D_MODEL // tn
    last = seq // tm - 1

    def first_pass(j, i):
        return jnp.where(j == 0, i, last)

    return pl.pallas_call(
        _mix_out_body,
        grid=(D_MODEL // tn, seq // tm),
        in_specs=[pl.BlockSpec((tm, D_SSM), lambda j, i: (first_pass(j, i), 0)),
                  pl.BlockSpec((tm, D_SSM), lambda j, i: (first_pass(j, i), Z_GS // D_SSM)),
                  pl.BlockSpec((tm, D_ATT), lambda j, i: (i, 0)),
                  pl.BlockSpec((None, D_SSM, D_SSM), lambda j, i: (l, 0, 0), pipeline_mode=pl.Buffered(1)),
                  pl.BlockSpec((None, 1, D_SSM), lambda j, i: (l, 0, 0)),
                  pl.BlockSpec((None, None, D_SSM, tn), lambda j, i: (l, 0, 0, j)),
                  pl.BlockSpec((None, None, D_ATT, tn), lambda j, i: (l, 1, 0, j)),
                  pl.BlockSpec((None, tm, tn), lambda j, i: (kx, i, j)),
                  pl.BlockSpec((None, None, 1, tn), lambda j, i: (l, kb, 0, gate_blk + j))],
        out_specs=pl.BlockSpec((tm, tn), lambda j, i: (i, j)),
        out_shape=jax.ShapeDtypeStruct((seq, D_MODEL), F32),
        scratch_shapes=[pltpu.VMEM((seq // tm, tm, D_SSM), BF16)],
        compiler_params=_params("arbitrary", "arbitrary"),
        name="mix_out",
    )(yg, z, oa, w_glu_b, b_glu3, w_out_b4, w_out_b4, x3, mod4)


def _final_norm_body(x_ref, g_ref, o_ref):
    x = x_ref[...]
    y = x * lax.rsqrt(jnp.mean(x * x, axis=-1, keepdims=True) + EPS)
    o_ref[...] = y * g_ref[...]


def _final_norm(x2, final_g2, tm):
    m = x2.shape[0]
    return pl.pallas_call(
        _final_norm_body,
        grid=(m // tm,),
        in_specs=[pl.BlockSpec((tm, D_MODEL), lambda i: (i, 0)),
                  pl.BlockSpec((1, D_MODEL), lambda i: (0, 0))],
        out_specs=pl.BlockSpec((tm, D_MODEL), lambda i: (i, 0)),
        out_shape=jax.ShapeDtypeStruct((m, D_MODEL), F32),
        compiler_params=_params("arbitrary"),
        name="final_norm",
    )(x2, final_g2)


def _final_norm_seqs_body(*refs):
    *x_refs, g_ref, o_ref = refs
    b = pl.program_id(0)
    x = x_refs[-1][...]
    for k in range(len(x_refs) - 2, -1, -1):
        x = jnp.where(b == k, x_refs[k][...], x)
    o_ref[...] = x * lax.rsqrt(jnp.mean(x * x, axis=-1, keepdims=True) + EPS) * g_ref[...]


def _final_norm_seqs(xs, final_g2, tm):
    seq = xs[0].shape[0]
    last = seq // tm - 1

    def xspec(k):
        return pl.BlockSpec((tm, D_MODEL), lambda b, i: (jnp.where(b == k, i, jnp.where(b > k, last, 0)), 0))

    return pl.pallas_call(
        _final_norm_seqs_body,
        grid=(len(xs), seq // tm),
        in_specs=[xspec(k) for k in range(len(xs))] + [pl.BlockSpec((1, D_MODEL), lambda b, i: (0, 0))],
        out_specs=pl.BlockSpec((None, tm, D_MODEL), lambda b, i: (b, i, 0)),
        out_shape=jax.ShapeDtypeStruct((len(xs), seq, D_MODEL), F32),
        compiler_params=_params("arbitrary", "arbitrary"),
        name="final_norm_seqs",
    )(*xs, final_g2)


def _rope_tables(pos):
    half = HEAD_DIM // 2
    inv = ROPE_THETA ** (-jnp.arange(half, dtype=F32) / half)
    ang = pos.astype(F32)[:, None] * inv[None, :]
    cos, sin = jnp.cos(ang), jnp.sin(ang)
    return jnp.concatenate([cos, cos], axis=-1), jnp.concatenate([-sin, sin], axis=-1)


def kernel(x_prompt, x_sample, c_prompt, c_sample, cache_win_k, cache_win_v, state_ssm_re, state_ssm_im, w_ada, b_ada, norm_g, w_in, ssm_a_re, ssm_a_im, ssm_log_dt, ssm_b_re, ssm_b_im, ssm_c_re, ssm_c_im, ssm_d, w_glu, b_glu, attn_sinks, w_out, final_g):
    bp, seq, _ = x_prompt.shape
    bs = x_sample.shape[0]
    past_len = 8192
    assert seq % (CHUNK * 128) == 0 and x_sample.shape[1] == 1 and cache_win_k.shape[2] == WINDOW
    assert bp + bs <= MOD_ROWS

    w_in_b = w_in.astype(BF16)
    w_glu_b = w_glu.astype(BF16)
    w_out_b4 = w_out.astype(BF16).reshape(DEPTH, 2, D_SSM, D_MODEL)
    norm_g3 = norm_g.reshape(DEPTH, 1, D_MODEL)
    b_glu3 = b_glu.reshape(DEPTH, 1, D_SSM)
    ssm_d3 = ssm_d.reshape(DEPTH, 1, D_SSM)

    c_all = jnp.concatenate([c_prompt, c_sample, jnp.zeros((MOD_ROWS - bp - bs, D_MODEL), F32)], axis=0)
    mod = _modulation(c_all, w_ada, b_ada)
    mod_p = mod[:, :bp].reshape(DEPTH, bp, 1, 3 * D_MODEL)
    mod_s = mod[:, bp:bp + bs].reshape(DEPTH, 1, bs, 3 * D_MODEL)

    prep = _ssm_prep(ssm_a_re, ssm_a_im, ssm_log_dt, ssm_b_re, ssm_b_im, ssm_c_re, ssm_c_im)
    cos_p, sin_p = _rope_tables(jnp.arange(seq))
    cos_s, sin_s = _rope_tables(jnp.full((bs,), past_len))

    xcur = [(x_prompt, k) for k in range(bp)]
    pk = [[None] * bp for _ in range(DEPTH)]
    pv = [[None] * bp for _ in range(DEPTH)]
    pr = [[None] * bp for _ in range(DEPTH)]
    pi = [[None] * bp for _ in range(DEPTH)]

    def finish(z, l, k, sa):
        yg, hr, hi, oa = sa
        x_new = _mix_out(yg, z, oa, w_glu_b, b_glu3, w_out_b4, xcur[k][0], xcur[k][1], mod_p, k, l)
        xcur[k] = (x_new[None], 0)
        pk[l][k] = z[seq - WINDOW:, Z_K:Z_K + D_KV].astype(F32).reshape(WINDOW, N_KV_HEADS, HEAD_DIM)
        pv[l][k] = z[seq - WINDOW:, Z_V:Z_V + D_KV].astype(F32).reshape(WINDOW, N_KV_HEADS, HEAD_DIM)
        pr[l][k] = hr.reshape(N_GROUPS, SSM_STATE)
        pi[l][k] = hi.reshape(N_GROUPS, SSM_STATE)

    pending = None
    for l in range(DEPTH):
        for k in range(bp):
            h = _rms_mod(xcur[k][0], xcur[k][1], norm_g3, mod_p, k, l, 256)
            if pending is None:
                z, = _prompt_stage(h, l, w_in_b, cos_p, sin_p, None, None, prep, ssm_d3, attn_sinks)
            else:
                z, *sa = _prompt_stage(h, l, w_in_b, cos_p, sin_p, pending[0], pending[1], prep, ssm_d3,
                                       attn_sinks)
                finish(*pending, sa)
            pending = (z, l, k)
    sa = _prompt_stage(None, None, w_in_b, cos_p, sin_p, pending[0], pending[1], prep, ssm_d3, attn_sinks)
    finish(*pending, sa)
    y_prompt = _final_norm_seqs([x[0] for x, _ in xcur], final_g.reshape(1, D_MODEL), 256)
    pk, pv, pr, pi = (jnp.stack([jnp.stack(row) for row in t]) for t in (pk, pv, pr, pi))

    xs = x_sample.reshape(bs, D_MODEL)
    sk, sv, sr, si = [], [], [], []
    for l in range(DEPTH):
        h = _rms_mod(xs[None], 0, norm_g3, mod_s, 0, l, bs)
        u, zb = _in_proj(h, w_in_b, cos_s, sin_s, l, bs)
        ut = u.reshape(bs, N_GROUPS, SSM_GROUP).transpose(1, 2, 0)
        ht_r = state_ssm_re[l].transpose(1, 2, 0)
        ht_i = state_ssm_im[l].transpose(1, 2, 0)
        ygt, nhr, nhi = _ssm_step(ut, ht_r, ht_i, prep, ssm_b_re, ssm_b_im, ssm_c_re, ssm_c_im, ssm_d, l)
        yg = ygt.transpose(2, 0, 1).reshape(bs, D_SSM).astype(BF16)
        zf = zb.astype(F32)
        q3 = zf[:, ZB_Q:ZB_Q + D_ATT].reshape(bs, N_HEADS, HEAD_DIM)
        ga3 = zf[:, ZB_GA:ZB_GA + D_ATT].reshape(bs, N_HEADS, HEAD_DIM)
        kn = zf[:, ZB_K:ZB_K + D_KV].reshape(bs, N_KV_HEADS, HEAD_DIM)
        vn = zf[:, ZB_V:ZB_V + D_KV].reshape(bs, N_KV_HEADS, HEAD_DIM)
        oa = _attention_step(q3, ga3, kn, vn, cache_win_k, cache_win_v, attn_sinks, l)
        ys = _glu(yg, zb, ZB_GS, w_glu_b, b_glu3, l, bs)
        xs = _out_proj(ys, oa.reshape(bs, D_ATT).astype(BF16), w_out_b4, xs[None], 0, mod_s, 0, l, bs)
        sk.append(kn.reshape(bs, 1, N_KV_HEADS, HEAD_DIM))
        sv.append(vn.reshape(bs, 1, N_KV_HEADS, HEAD_DIM))
        sr.append(nhr.transpose(2, 0, 1))
        si.append(nhi.transpose(2, 0, 1))
    y_sample = _final_norm(xs, final_g.reshape(1, D_MODEL), bs).reshape(bs, 1, D_MODEL)
    new_k, new_v = _roll_cache(cache_win_k, cache_win_v, jnp.stack(sk), jnp.stack(sv))

    return (y_prompt, y_sample, jnp.stack(pk), jnp.stack(pv), jnp.stack(pr), jnp.stack(pi),
            new_k, new_v, jnp.stack(sr), jnp.stack(si))
```

```python
import functools
import math

import jax
import jax.numpy as jnp
from jax import lax
from jax.experimental import pallas as pl
from jax.experimental.pallas import tpu as pltpu

F32 = jnp.float32
BF16 = jnp.bfloat16

D_MODEL = 4096
DEPTH = 4
D_SSM = 2048
D_ATT = 2048
SSM_GROUP = 16
N_GROUPS = 128
SSM_STATE = 64
HEAD_DIM = 128
N_HEADS = 16
N_KV_HEADS = 4
KV_GROUP = 4
D_KV = 512
WINDOW = 128
ROPE_THETA = 10000.0
EPS = 1e-6
BLOCK = 128
D_IN = 9216
ATT_SCALE = HEAD_DIM ** -0.5

IN_TN = 512
IN_TILES = D_IN // IN_TN
U_TILES = D_SSM // IN_TN
ZB_COLS = D_IN - D_SSM
ZB_Q, ZB_GA, ZB_GS, ZB_K, ZB_V = 0, 2048, 4096, 6144, 6656

CHUNK = 16
SSM_GB = 8
SSM_W = SSM_GB * SSM_GROUP
SCAN_LANES = 128
MOD_ROWS = 40

VMEM_LIMIT_BYTES = 56 * 1024 * 1024


def _params(*sem, vmem_limit_bytes=VMEM_LIMIT_BYTES):
    return pltpu.CompilerParams(dimension_semantics=sem, vmem_limit_bytes=vmem_limit_bytes)


def _mod_body(c_ref, w_ref, b_ref, o_ref):
    a = jax.nn.silu(c_ref[...]).astype(BF16)
    o_ref[...] = jnp.dot(a, w_ref[...].astype(BF16), preferred_element_type=F32) + b_ref[...]


def _modulation(c_all, w_ada, b_ada):
    tn = 512
    return pl.pallas_call(
        _mod_body,
        grid=(DEPTH, 3 * D_MODEL // tn),
        in_specs=[pl.BlockSpec((MOD_ROWS, D_MODEL), lambda l, j: (0, 0)),
                  pl.BlockSpec((None, D_MODEL, tn), lambda l, j: (l, 0, j)),
                  pl.BlockSpec((None, 1, tn), lambda l, j: (l, 0, j))],
        out_specs=pl.BlockSpec((None, MOD_ROWS, tn), lambda l, j: (l, 0, j)),
        out_shape=jax.ShapeDtypeStruct((DEPTH, MOD_ROWS, 3 * D_MODEL), F32),
        compiler_params=_params("arbitrary", "arbitrary"),
        name="adaln_mod",
    )(c_all, w_ada, b_ada.reshape(DEPTH, 1, 3 * D_MODEL))


def _rmsmod_body(x_ref, g_ref, sc_ref, sh_ref, o_ref):
    x = x_ref[...]
    y = x * lax.rsqrt(jnp.mean(x * x, axis=-1, keepdims=True) + EPS)
    o_ref[...] = ((y * g_ref[...]) * (1.0 + sc_ref[...]) + sh_ref[...]).astype(BF16)


def _rms_mod(x3, kx, norm_g3, mod4, kb, l, tm):
    lr = x3.shape[1]
    r = mod4.shape[2]
    return pl.pallas_call(
        _rmsmod_body,
        grid=(lr // tm,),
        in_specs=[pl.BlockSpec((None, tm, D_MODEL), lambda i: (kx, i, 0)),
                  pl.BlockSpec((None, 1, D_MODEL), lambda i: (l, 0, 0)),
                  pl.BlockSpec((None, None, r, D_MODEL), lambda i: (l, kb, 0, 1)),
                  pl.BlockSpec((None, None, r, D_MODEL), lambda i: (l, kb, 0, 0))],
        out_specs=pl.BlockSpec((tm, D_MODEL), lambda i: (i, 0)),
        out_shape=jax.ShapeDtypeStruct((lr, D_MODEL), BF16),
        compiler_params=_params("arbitrary"),
        name="rms_mod",
    )(x3, norm_g3, mod4, mod4)


def _w_in_tile(j):
    return jnp.where(j < 4, j, jnp.where(j < 8, j + 4, jnp.where(j < 12, j + 6, jnp.where(j < 16, j - 8, j - 4))))


def _rope(x, cos, sin):
    return x * cos + pltpu.roll(x, HEAD_DIM // 2, 1) * sin


def _inproj_body(h_ref, w_ref, cos_ref, sin_ref, u_ref, zb_ref):
    j = pl.program_id(1)
    half = IN_TN // 2

    def halves(epilogue):
        for c in range(2):
            sl = slice(c * half, (c + 1) * half)
            epilogue(sl, jnp.dot(h_ref[...], w_ref[:, sl], preferred_element_type=F32))

    @pl.when(j < 4)
    def _():
        def plain(sl, acc):
            u_ref[:, sl] = acc
        halves(plain)

    @pl.when(((j >= 4) & (j < 8)) | (j == 16))
    def _():
        def rope(sl, acc):
            for hh in range(half // HEAD_DIM):
                hs = slice(hh * HEAD_DIM, (hh + 1) * HEAD_DIM)
                zb_ref[:, sl.start + hs.start:sl.start + hs.stop] = _rope(
                    acc[:, hs], cos_ref[...], sin_ref[...]).astype(BF16)
        halves(rope)

    @pl.when((j >= 8) & (j < 16))
    def _():
        def silu(sl, acc):
            zb_ref[:, sl] = jax.nn.silu(acc).astype(BF16)
        halves(silu)

    @pl.when(j == 17)
    def _():
        def cast(sl, acc):
            zb_ref[:, sl] = acc.astype(BF16)
        halves(cast)


def _in_proj(h2, w_in_b, cos_t, sin_t, l, tm):
    m = h2.shape[0]
    nt = cos_t.shape[0] // tm
    return pl.pallas_call(
        _inproj_body,
        grid=(m // tm, IN_TILES),
        in_specs=[pl.BlockSpec((tm, D_MODEL), lambda i, j: (i, 0)),
                  pl.BlockSpec((None, D_MODEL, IN_TN), lambda i, j: (l, 0, _w_in_tile(j))),
                  pl.BlockSpec((tm, HEAD_DIM), lambda i, j: (i % nt, 0)),
                  pl.BlockSpec((tm, HEAD_DIM), lambda i, j: (i % nt, 0))],
        out_specs=[pl.BlockSpec((tm, IN_TN), lambda i, j: (i, jnp.minimum(j, U_TILES - 1))),
                   pl.BlockSpec((tm, IN_TN), lambda i, j: (i, jnp.maximum(j - U_TILES, 0)))],
        out_shape=[jax.ShapeDtypeStruct((m, D_SSM), F32),
                   jax.ShapeDtypeStruct((m, ZB_COLS), BF16)],
        compiler_params=_params("arbitrary", "arbitrary"),
        name="in_proj",
    )(h2, w_in_b, cos_t, sin_t)


def _cmul(ar, ai, br, bi):
    return ar * br - ai * bi, ar * bi + ai * br


def _discretise(ar, ai, log_dt):
    dt = jnp.exp(log_dt)
    mag = jnp.exp(ar * dt)
    lr = mag * jnp.cos(ai * dt)
    li = mag * jnp.sin(ai * dt)
    nr = lr - 1.0
    ni = li
    den = ar * ar + ai * ai
    qr = (nr * ar + ni * ai) / den
    qi = (ni * ar - nr * ai) / den
    return lr, li, qr, qi


def _powers(lr, li, n):
    pr, pi = [jnp.ones_like(lr)], [jnp.zeros_like(li)]
    for _ in range(n):
        r, i = _cmul(pr[-1], pi[-1], lr, li)
        pr.append(r)
        pi.append(i)
    return pr, pi


PREP_GB = 16


def _split_bf16(x):
    hi = x.astype(BF16)
    return hi, (x - hi.astype(F32)).astype(BF16)


def _ssm_prep_body(a2r_ref, a2i_ref, ldt_ref, cc_ref, cs_ref, bc_ref, bs_ref,
                   mt_ref, ws_ref, wc_ref, l16r_ref, l16i_ref, l1r_ref, l1i_ref, qr_ref, qi_ref):
    gb, t_len, p = PREP_GB, CHUNK, SSM_STATE
    kdim = t_len * SSM_GROUP
    lr, li, qr, qi = _discretise(a2r_ref[...], a2i_ref[...], ldt_ref[...])
    sgn = jnp.where(lax.broadcasted_iota(jnp.int32, (1, 2 * p), 1) < p, 1.0, -1.0)
    pr, pi = _powers(lr, li, t_len)
    mag2 = lr * lr + li * li
    ir, ii = _powers(lr / mag2, -li / mag2, t_len - 1)
    prs = [x * sgn for x in pr]
    w_inv = [_cmul(qr, qi, ir[s], ii[s]) for s in range(t_len)]
    w_15 = [_cmul(qr, qi, pr[t_len - 1 - s], pi[t_len - 1 - s]) for s in range(t_len)]
    w_inv = [(wr, wi * sgn) for wr, wi in w_inv]
    w_15 = [(wr, wi * sgn) for wr, wi in w_15]

    row_t = lax.broadcasted_iota(jnp.int32, (kdim, kdim), 0) >> 4
    col_s = lax.broadcasted_iota(jnp.int32, (kdim, kdim), 1) >> 4
    causal = row_t >= col_s
    nt = (((1,), (1,)), ((), ()))
    for g in range(gb):
        row = slice(g, g + 1)
        cc, cs = cc_ref[g], cs_ref[g]
        bc, bs = bc_ref[g], bs_ref[g]
        cl = jnp.concatenate([cc * prs[t][row] - cs * pi[t][row] for t in range(t_len)], axis=0)
        wc = jnp.concatenate([cc * prs[t + 1][row] - cs * pi[t + 1][row] for t in range(t_len)], axis=0)
        bl = jnp.concatenate([bc * w_inv[s][0][row] - bs * w_inv[s][1][row] for s in range(t_len)], axis=0)
        ws = jnp.concatenate([bc * w_15[s][0][row] - bs * w_15[s][1][row] for s in range(t_len)], axis=0)
        a_hi, a_lo = _split_bf16(cl)
        b_hi, b_lo = _split_bf16(bl)
        m = (lax.dot_general(a_hi, b_hi, nt, preferred_element_type=F32)
             + lax.dot_general(a_hi, b_lo, nt, preferred_element_type=F32)
             + lax.dot_general(a_lo, b_hi, nt, preferred_element_type=F32))
        mt_ref[g] = jnp.where(causal, m, 0.0).astype(BF16)
        ws_ref[g] = ws.T.astype(BF16)
        wc_ref[g] = wc.astype(BF16)

    pad = jnp.zeros((2 * p - gb, 2 * p), F32)
    outs = ((pr[t_len], l16r_ref), (pi[t_len], l16i_ref), (lr, l1r_ref), (li, l1i_ref),
            (qr, qr_ref), (qi, qi_ref))
    for val, ref in outs:
        col = jnp.concatenate([val, pad], axis=0).T
        for g in range(gb):
            ref[g] = jnp.broadcast_to(col[:p, g:g + 1], ref.shape[1:])


def _ssm_prep(a_re, a_im, log_dt, b_re, b_im, c_re, c_im):
    g, p, kdim, gb = N_GROUPS, SSM_STATE, CHUNK * SSM_GROUP, PREP_GB
    a2r = jnp.concatenate([a_re, a_re], axis=-1)
    a2i = jnp.concatenate([a_im, a_im], axis=-1)
    ldt = log_dt.reshape(DEPTH, g, 1)
    bt_re = jnp.swapaxes(b_re, -1, -2)
    bt_im = jnp.swapaxes(b_im, -1, -2)
    cc = jnp.concatenate([c_re, c_im], axis=-1)
    cs = jnp.concatenate([c_im, c_re], axis=-1)
    bc = jnp.concatenate([bt_re, bt_im], axis=-1)
    bs = jnp.concatenate([bt_im, bt_re], axis=-1)

    def spec(a, b):
        return pl.BlockSpec((None, gb, a, b), lambda l, i: (l, i, 0, 0))

    def spec2(b):
        return pl.BlockSpec((None, gb, b), lambda l, i: (l, i, 0))

    col_shape = jax.ShapeDtypeStruct((DEPTH, g, p, 1), F32)
    wide_shape = jax.ShapeDtypeStruct((DEPTH, g, p, SCAN_LANES), F32)
    return pl.pallas_call(
        _ssm_prep_body,
        grid=(DEPTH, g // gb),
        in_specs=[spec2(2 * p), spec2(2 * p), spec2(1),
                  spec(SSM_GROUP, 2 * p), spec(SSM_GROUP, 2 * p), spec(SSM_GROUP, 2 * p),
                  spec(SSM_GROUP, 2 * p)],
        out_specs=([spec(kdim, kdim), spec(2 * p, kdim), spec(kdim, 2 * p)]
                   + [spec(p, SCAN_LANES)] * 2 + [spec(p, 1)] * 4),
        out_shape=[jax.ShapeDtypeStruct((DEPTH, g, kdim, kdim), BF16),
                   jax.ShapeDtypeStruct((DEPTH, g, 2 * p, kdim), BF16),
                   jax.ShapeDtypeStruct((DEPTH, g, kdim, 2 * p), BF16)] + [wide_shape] * 2 + [col_shape] * 4,
        compiler_params=_params("arbitrary", "arbitrary"),
        name="ssm_prep",
    )(a2r, a2i, ldt, cc, cs, bc, bs)


def _ssm_part(u_ref, mt_ref, ws_ref, wc_ref, l16r_ref, l16i_ref, d_ref,
              yg_ref, hr_ref, hi_ref, ubuf, ybuf):
    gb, t_len, p = SSM_GB, CHUNK, SSM_STATE
    r_len = u_ref.shape[0] // t_len
    ubuf[...] = u_ref[...].astype(F32)
    slab_t = [ubuf[pl.ds(s, r_len, stride=t_len), :].T.astype(BF16) for s in range(t_len)]

    yl, xr, xi = [], [], []
    for g in range(gb):
        rows = slice(g * SSM_GROUP, (g + 1) * SSM_GROUP)
        s_g = jnp.concatenate([st[rows, :] for st in slab_t], axis=0)
        both = jnp.dot(jnp.concatenate([mt_ref[g], ws_ref[g]], axis=0), s_g, preferred_element_type=F32)
        kd = t_len * SSM_GROUP
        yl.append(both[:kd, :])
        xr.append(both[kd:kd + p, :])
        xi.append(both[kd + p:, :])
    h_r = jnp.concatenate(xr, axis=0)
    h_i = jnp.concatenate(xi, axis=0)
    yield

    a_r = l16r_ref[...].reshape(gb * p, r_len)
    a_i = l16i_ref[...].reshape(gb * p, r_len)
    lane = lax.broadcasted_iota(jnp.int32, (gb * p, r_len), 1)
    k = 1
    while k < r_len:
        keep = lane >= k
        s_r = jnp.where(keep, pltpu.roll(h_r, k, 1), 0.0)
        s_i = jnp.where(keep, pltpu.roll(h_i, k, 1), 0.0)
        h_r, h_i = h_r + a_r * s_r - a_i * s_i, h_i + a_r * s_i + a_i * s_r
        a_r, a_i = a_r * a_r - a_i * a_i, 2.0 * a_r * a_i
        k *= 2
    hr_ref[...] = h_r[:, r_len - 1:r_len].reshape(gb, p, 1)
    hi_ref[...] = h_i[:, r_len - 1:r_len].reshape(gb, p, 1)

    first = lane >= 1
    c_r = jnp.where(first, pltpu.roll(h_r, 1, 1), 0.0).astype(BF16)
    c_i = jnp.where(first, pltpu.roll(h_i, 1, 1), 0.0).astype(BF16)
    ys = []
    for g in range(gb):
        rows = slice(g * p, (g + 1) * p)
        c_g = jnp.concatenate([c_r[rows, :], c_i[rows, :]], axis=0)
        ys.append(yl[g] + jnp.dot(wc_ref[g], c_g, preferred_element_type=F32))
    yield
    d = d_ref[...]
    for t in range(t_len):
        rows = slice(t * SSM_GROUP, (t + 1) * SSM_GROUP)
        y_t = jnp.concatenate([y[rows, :] for y in ys], axis=0)
        u_t = ubuf[pl.ds(t, r_len, stride=t_len), :]
        ybuf[pl.ds(t, r_len, stride=t_len), :] = jax.nn.gelu(y_t.T + d * u_t)
    yg_ref[...] = ybuf[...].astype(BF16)


SSM_STEP_GB = 16


def _ssm_step_body(lr_ref, li_ref, qr_ref, qi_ref, br_ref, bi_ref, cr_ref, ci_ref, d_ref, ut_ref,
                   hr_ref, hi_ref, yt_ref, nhr_ref, nhi_ref):
    lr, li = lr_ref[...], li_ref[...]
    bbr, bbi = _cmul(qr_ref[...], qi_ref[...], br_ref[...], bi_ref[...])
    for g in range(SSM_STEP_GB):
        u = ut_ref[g]
        ub = u.astype(BF16)
        x_r = jnp.dot(bbr[g].astype(BF16), ub, preferred_element_type=F32)
        x_i = jnp.dot(bbi[g].astype(BF16), ub, preferred_element_type=F32)
        h_r, h_i = hr_ref[g], hi_ref[g]
        n_r = lr[g] * h_r - li[g] * h_i + x_r
        n_i = lr[g] * h_i + li[g] * h_r + x_i
        y = (jnp.dot(cr_ref[g].astype(BF16), n_r.astype(BF16), preferred_element_type=F32)
             - jnp.dot(ci_ref[g].astype(BF16), n_i.astype(BF16), preferred_element_type=F32))
        yt_ref[g] = jax.nn.gelu(y + d_ref[g] * u)
        nhr_ref[g] = n_r
        nhi_ref[g] = n_i


def _ssm_step(ut, ht_r, ht_i, prep, b_re, b_im, c_re, c_im, ssm_d, l):
    l1r, l1i, qr, qi = prep[5:9]
    g, p, gb = N_GROUPS, SSM_STATE, SSM_STEP_GB
    bs = ut.shape[-1]

    def lspec(a, c):
        return pl.BlockSpec((None, gb, a, c), lambda i: (l, i, 0, 0))

    def aspec(a):
        return pl.BlockSpec((gb, a, bs), lambda i: (i, 0, 0))

    return pl.pallas_call(
        _ssm_step_body,
        grid=(g // gb,),
        in_specs=[lspec(p, 1), lspec(p, 1), lspec(p, 1), lspec(p, 1), lspec(p, SSM_GROUP), lspec(p, SSM_GROUP),
                  lspec(SSM_GROUP, p), lspec(SSM_GROUP, p), lspec(SSM_GROUP, 1),
                  aspec(SSM_GROUP), aspec(p), aspec(p)],
        out_specs=[aspec(SSM_GROUP), aspec(p), aspec(p)],
        out_shape=[jax.ShapeDtypeStruct((g, SSM_GROUP, bs), F32),
                   jax.ShapeDtypeStruct((g, p, bs), F32),
                   jax.ShapeDtypeStruct((g, p, bs), F32)],
        compiler_params=_params("arbitrary"),
        name="ssm_step",
    )(l1r, l1i, qr, qi, b_re, b_im, c_re, c_im, ssm_d.reshape(DEPTH, g, SSM_GROUP, 1), ut, ht_r, ht_i)


def _attn_part(l, n, q_ref, ga_ref, kc_ref, kp_ref, vc_ref, vp_ref, sink_ref, o_ref):
    cols = KV_GROUP * BLOCK
    j = lax.broadcasted_iota(jnp.int32, (2 * BLOCK, cols), 0)
    r = lax.broadcasted_iota(jnp.int32, (2 * BLOCK, cols), 1) & (BLOCK - 1)
    no_prev = jnp.where(n > 0, 0, 2 * BLOCK)
    valid = ((j < BLOCK) & (j > r + no_prev)) | ((j >= BLOCK) & (j - BLOCK <= r))
    ones = jnp.ones((HEAD_DIM, 2 * BLOCK), BF16)
    nt = (((1,), (1,)), ((), ()))
    heads = [[kh * KV_GROUP + g for g in range(KV_GROUP)] for kh in range(N_KV_HEADS)]
    scores = []
    for kh in range(N_KV_HEADS):
        ksl = slice(kh * HEAD_DIM, (kh + 1) * HEAD_DIM)
        k = jnp.concatenate([kp_ref[:, ksl], kc_ref[:, ksl]], axis=0)
        q4 = jnp.concatenate([q_ref[:, h * HEAD_DIM:(h + 1) * HEAD_DIM] for h in heads[kh]], axis=0)
        scores.append(lax.dot_general(k, q4, nt, preferred_element_type=F32))
    yield
    outs = []
    for kh in range(N_KV_HEADS):
        ksl = slice(kh * HEAD_DIM, (kh + 1) * HEAD_DIM)
        vt = jnp.concatenate([vp_ref[:, ksl].T, vc_ref[:, ksl].T], axis=1)
        v1t = jnp.concatenate([vt, ones], axis=0)
        st = jnp.where(valid, scores[kh] * ATT_SCALE, -jnp.inf)
        sink = jnp.concatenate([jnp.full((1, BLOCK), sink_ref[l, h], F32) for h in heads[kh]], axis=1)
        m = jnp.maximum(jnp.max(st, axis=0, keepdims=True), sink)
        e = jnp.exp(st - m).astype(BF16)
        o1 = jnp.dot(v1t, e, preferred_element_type=F32)
        outs.append((o1, jnp.exp(sink - m)))
    yield
    for kh in range(N_KV_HEADS):
        o1, e_sink = outs[kh]
        ot = o1[:HEAD_DIM, :] / (o1[HEAD_DIM:, :] + e_sink)
        for g, h in enumerate(heads[kh]):
            hsl = slice(h * HEAD_DIM, (h + 1) * HEAD_DIM)
            o_ref[:, hsl] = (ot[:, g * BLOCK:(g + 1) * BLOCK].T * ga_ref[:, hsl].astype(F32)).astype(BF16)


Z_Q, Z_GA, Z_GS, Z_K, Z_V = 2048, 4096, 6144, 8192, 8704
SA_STEPS = N_GROUPS // SSM_GB
IN_ROWS = 512


def _in_part(j, h_ref, w_ref, cos_ref, sin_ref, z_ref):
    is_rope = ((j >= 4) & (j < 8)) | (j == 16)
    is_silu = (j >= 8) & (j < 16)
    for rc in range(h_ref.shape[0] // IN_ROWS):
        rs = slice(rc * IN_ROWS, (rc + 1) * IN_ROWS)
        acc = jnp.dot(h_ref[rs, :], w_ref[...], preferred_element_type=F32)
        cos, sin = cos_ref[rs, :], sin_ref[rs, :]
        roped = jnp.concatenate([_rope(acc[:, hh * HEAD_DIM:(hh + 1) * HEAD_DIM], cos, sin)
                                 for hh in range(IN_TN // HEAD_DIM)], axis=1)
        out = jnp.where(is_rope, roped, jnp.where(is_silu, jax.nn.silu(acc), acc))
        z_ref[rs, :] = out.astype(BF16)
        yield


def _stage_body(has_in, has_sa, l_prev, *refs):
    j = pl.program_id(0)
    refs = list(refs)
    in_refs = [refs.pop(0) for _ in range(4)] if has_in else []
    sa_refs = [refs.pop(0) for _ in range(14)] if has_sa else []
    proj = _in_part(j, *in_refs, refs.pop(0)) if has_in else iter(())
    side = []
    if has_sa:
        u_ref, q_ref, ga_ref, kc_ref, kp_ref, vc_ref, vp_ref = sa_refs[:7]
        mt_ref, ws_ref, wc_ref, l16r_ref, l16i_ref, d_ref, sink_ref = sa_refs[7:]
        yg_ref, hr_ref, hi_ref, oa_ref, ubuf, ybuf = refs
        side = [_ssm_part(u_ref, mt_ref, ws_ref, wc_ref, l16r_ref, l16i_ref, d_ref, yg_ref, hr_ref,
                          hi_ref, ubuf, ybuf),
                _attn_part(l_prev, jnp.minimum(j, SA_STEPS - 1), q_ref, ga_ref, kc_ref, kp_ref, vc_ref,
                           vp_ref, sink_ref, oa_ref)]
    live = True
    while live:
        live = False
        for gen in side + [proj]:
            if next(gen, "done") != "done":
                live = True


def _prompt_stage(h, l_in, w_in_b, cos_t, sin_t, z_prev, l_prev, prep, ssm_d3, sinks):
    has_in, has_sa = h is not None, z_prev is not None
    seq = (h if has_in else z_prev).shape[0]
    gb, p, kdim = SSM_GB, SSM_STATE, CHUNK * SSM_GROUP
    in_specs, args, out_specs, out_shape, scratch = [], [], [], [], []

    def jm(j):
        return jnp.minimum(j, SA_STEPS - 1)

    if has_in:
        table = pl.BlockSpec((seq, HEAD_DIM), lambda j: (0, 0))
        in_specs += [pl.BlockSpec((seq, D_MODEL), lambda j: (0, 0), pipeline_mode=pl.Buffered(1)),
                     pl.BlockSpec((None, D_MODEL, IN_TN), lambda j: (l_in, 0, _w_in_tile(j))), table, table]
        args += [h, w_in_b, cos_t, sin_t]
        out_specs.append(pl.BlockSpec((seq, IN_TN), lambda j: (0, j)))
        out_shape.append(jax.ShapeDtypeStruct((seq, D_IN), BF16))
    if has_sa:
        def wspec(a, c):
            return pl.BlockSpec((None, gb, a, c), lambda j: (l_prev, jm(j), 0, 0))

        def kv(col, prev):
            if prev:
                return pl.BlockSpec((BLOCK, D_KV), lambda j: (jnp.maximum(jm(j) - 1, 0), col // D_KV))
            return pl.BlockSpec((BLOCK, D_KV), lambda j: (jm(j), col // D_KV))

        st_spec = pl.BlockSpec((gb, p, 1), lambda j: (jm(j), 0, 0))
        in_specs += [pl.BlockSpec((seq, SSM_W), lambda j: (0, jm(j))),
                     pl.BlockSpec((BLOCK, D_ATT), lambda j: (jm(j), Z_Q // D_ATT)),
                     pl.BlockSpec((BLOCK, D_ATT), lambda j: (jm(j), Z_GA // D_ATT)),
                     kv(Z_K, False), kv(Z_K, True), kv(Z_V, False), kv(Z_V, True),
                     wspec(kdim, kdim), wspec(2 * p, kdim), wspec(kdim, 2 * p),
                     wspec(p, SCAN_LANES), wspec(p, SCAN_LANES),
                     pl.BlockSpec((None, 1, SSM_W), lambda j: (l_prev, 0, jm(j))),
                     pl.BlockSpec(memory_space=pltpu.SMEM)]
        args += [z_prev] * 7 + list(prep[:5]) + [ssm_d3, sinks]
        out_specs += [pl.BlockSpec((seq, SSM_W), lambda j: (0, jm(j))), st_spec, st_spec,
                      pl.BlockSpec((BLOCK, D_ATT), lambda j: (jm(j), 0))]
        out_shape += [jax.ShapeDtypeStruct((seq, D_SSM), BF16),
                      jax.ShapeDtypeStruct((N_GROUPS, p, 1), F32),
                      jax.ShapeDtypeStruct((N_GROUPS, p, 1), F32),
                      jax.ShapeDtypeStruct((seq, D_ATT), BF16)]
        scratch = [pltpu.VMEM((seq, SSM_W), F32), pltpu.VMEM((seq, SSM_W), F32)]
    return pl.pallas_call(
        functools.partial(_stage_body, has_in, has_sa, l_prev),
        grid=(IN_TILES if has_in else SA_STEPS,),
        in_specs=in_specs, out_specs=out_specs, out_shape=out_shape, scratch_shapes=scratch,
        compiler_params=_params("arbitrary"),
        name="stage" + ("_in" if has_in else "") + ("_sa" if has_sa else ""),
    )(*args)


STEP_SB = 8


def _attn_step_body(l, q_ref, ga_ref, kn_ref, vn_ref, ck_ref, cv_ref, sink_ref, o_ref):
    rows = WINDOW * N_KV_HEADS
    head = lax.broadcasted_iota(jnp.int32, (N_HEADS, rows), 0)
    col = lax.broadcasted_iota(jnp.int32, (N_HEADS, rows), 1)
    valid = ((col & (N_KV_HEADS - 1)) == (head >> 2)) & (col >= N_KV_HEADS)
    sink = jnp.concatenate([jnp.full((1, 1), sink_ref[l, h], F32) for h in range(N_HEADS)], axis=0)
    nt = (((1,), (1,)), ((), ()))

    def per_head(x4):
        return jnp.concatenate([jnp.broadcast_to(x4[kh:kh + 1, :], (KV_GROUP, HEAD_DIM))
                                for kh in range(N_KV_HEADS)], axis=0)

    for b in range(STEP_SB):
        q = q_ref[b].astype(BF16)
        kn = per_head(kn_ref[b]).astype(BF16).astype(F32)
        vn = per_head(vn_ref[b]).astype(BF16).astype(F32)
        s = lax.dot_general(q, ck_ref[b].astype(BF16), nt, preferred_element_type=F32) * ATT_SCALE
        s = jnp.where(valid, s, -jnp.inf)
        s_new = jnp.sum(q.astype(F32) * kn, axis=-1, keepdims=True) * ATT_SCALE
        m = jnp.maximum(jnp.maximum(jnp.max(s, axis=-1, keepdims=True), s_new), sink)
        e = jnp.exp(s - m)
        e_new = jnp.exp(s_new - m)
        den = jnp.sum(e, axis=-1, keepdims=True) + e_new + jnp.exp(sink - m)
        o = (jnp.dot((e / den).astype(BF16), cv_ref[b].astype(BF16), preferred_element_type=F32)
             + (e_new / den).astype(BF16).astype(F32) * vn)
        o_ref[b] = o * ga_ref[b]


def _attention_step(q3, ga3, kn3, vn3, cache_k, cache_v, sinks, l):
    bs = q3.shape[0]
    sb = STEP_SB
    rows = WINDOW * N_KV_HEADS
    hspec = pl.BlockSpec((sb, N_HEADS, HEAD_DIM), lambda i: (i, 0, 0))
    nspec = pl.BlockSpec((sb, N_KV_HEADS, HEAD_DIM), lambda i: (i, 0, 0))
    cspec = pl.BlockSpec((None, sb, rows, HEAD_DIM), lambda i: (l, i, 0, 0))
    ck = cache_k.reshape(DEPTH, bs, rows, HEAD_DIM)
    cv = cache_v.reshape(DEPTH, bs, rows, HEAD_DIM)
    return pl.pallas_call(
        functools.partial(_attn_step_body, l),
        grid=(bs // sb,),
        in_specs=[hspec, hspec, nspec, nspec, cspec, cspec, pl.BlockSpec(memory_space=pltpu.SMEM)],
        out_specs=hspec,
        out_shape=jax.ShapeDtypeStruct((bs, N_HEADS, HEAD_DIM), F32),
        compiler_params=_params("arbitrary"),
        name="swa_step",
    )(q3, ga3, kn3, vn3, ck, cv, sinks)


def _roll_cache_body(ck_ref, cv_ref, kn_ref, vn_ref, ok_ref, ov_ref):
    for c_ref, n_ref, o_ref in ((ck_ref, kn_ref, ok_ref), (cv_ref, vn_ref, ov_ref)):
        o_ref[:, :WINDOW - 1] = c_ref[:, 1:]
        o_ref[:, WINDOW - 1:] = n_ref[...]


def _roll_cache(cache_k, cache_v, k_new, v_new):
    bs = cache_k.shape[1]
    sb = STEP_SB
    cspec = pl.BlockSpec((None, sb, WINDOW, N_KV_HEADS, HEAD_DIM), lambda l, i: (l, i, 0, 0, 0))
    nspec = pl.BlockSpec((None, sb, 1, N_KV_HEADS, HEAD_DIM), lambda l, i: (l, i, 0, 0, 0))
    shape = jax.ShapeDtypeStruct(cache_k.shape, F32)
    return pl.pallas_call(
        _roll_cache_body,
        grid=(DEPTH, bs // sb),
        in_specs=[cspec, cspec, nspec, nspec],
        out_specs=[cspec, cspec],
        out_shape=[shape, shape],
        compiler_params=_params("arbitrary", "arbitrary"),
        name="roll_cache",
    )(cache_k, cache_v, k_new, v_new)


def _glu_body(y_ref, w_ref, b_ref, gs_ref, o_ref):
    y = y_ref[...]
    step = 512
    for c in range(D_SSM // step):
        sl = slice(c * step, (c + 1) * step)
        acc = jnp.dot(y, w_ref[:, sl], preferred_element_type=F32) + b_ref[:, sl]
        o_ref[:, sl] = (y_ref[:, sl].astype(F32) * jax.nn.sigmoid(acc)
                        * gs_ref[:, sl].astype(F32)).astype(BF16)


def _glu(yg2, zb2, gs_col, w_glu_b, b_glu3, l, tm):
    m = yg2.shape[0]
    return pl.pallas_call(
        _glu_body,
        grid=(m // tm,),
        in_specs=[pl.BlockSpec((tm, D_SSM), lambda i: (i, 0)),
                  pl.BlockSpec((None, D_SSM, D_SSM), lambda i: (l, 0, 0)),
                  pl.BlockSpec((None, 1, D_SSM), lambda i: (l, 0, 0)),
                  pl.BlockSpec((tm, D_SSM), lambda i: (i, gs_col // D_SSM))],
        out_specs=pl.BlockSpec((tm, D_SSM), lambda i: (i, 0)),
        out_shape=jax.ShapeDtypeStruct((m, D_SSM), BF16),
        compiler_params=_params("arbitrary"),
        name="glu",
    )(yg2, w_glu_b, b_glu3, zb2)


def _outproj_body(a_ref, b_ref, w1_ref, w2_ref, x_ref, g_ref, o_ref):
    acc = (jnp.dot(a_ref[...], w1_ref[...], preferred_element_type=F32)
           + jnp.dot(b_ref[...], w2_ref[...], preferred_element_type=F32))
    o_ref[...] = x_ref[...] + g_ref[...] * acc


def _out_proj(ys2, oa2, w_out_b4, x3, kx, mod4, kb, l, tm):
    m = x3.shape[1]
    tn = 1024
    r = mod4.shape[2]
    gate_blk = 2 * D_MODEL // tn
    return pl.pallas_call(
        _outproj_body,
        grid=(D_MODEL // tn, m // tm),
        in_specs=[pl.BlockSpec((tm, D_SSM), lambda j, i: (i, 0)),
                  pl.BlockSpec((tm, D_ATT), lambda j, i: (i, 0)),
                  pl.BlockSpec((None, None, D_SSM, tn), lambda j, i: (l, 0, 0, j)),
                  pl.BlockSpec((None, None, D_ATT, tn), lambda j, i: (l, 1, 0, j)),
                  pl.BlockSpec((None, tm, tn), lambda j, i: (kx, i, j)),
                  pl.BlockSpec((None, None, r, tn), lambda j, i: (l, kb, 0, gate_blk + j))],
        out_specs=pl.BlockSpec((tm, tn), lambda j, i: (i, j)),
        out_shape=jax.ShapeDtypeStruct((m, D_MODEL), F32),
        compiler_params=_params("arbitrary", "arbitrary"),
        name="out_proj",
    )(ys2, oa2, w_out_b4, w_out_b4, x3, mod4)


GO_TM = 512
GO_TN = 512


def _mix_out_body(yg_ref, gs_ref, oa_ref, wg_ref, bg_ref, w1_ref, w2_ref, x_ref, gate_ref, xo_ref, ys_buf):
    i = pl.program_id(1)

    @pl.when(pl.program_id(0) == 0)
    def _():
        _glu_body(yg_ref, wg_ref, bg_ref, gs_ref, ys_buf.at[i])

    acc = (jnp.dot(ys_buf[i], w1_ref[...], preferred_element_type=F32)
           + jnp.dot(oa_ref[...], w2_ref[...], preferred_element_type=F32))
    xo_ref[...] = x_ref[...] + gate_ref[...] * acc


def _mix_out(yg, z, oa, w_glu_b, b_glu3, w_out_b4, x3, kx, mod4, kb, l):
    seq = yg.shape[0]
    tm, tn = GO_TM, GO_TN
    gate_blk = 2 * D_MODEL // tn
    last = seq // tm - 1

    def first_pass(j, i):
        return jnp.where(j == 0, i, last)

    return pl.pallas_call(
        _mix_out_body,
        grid=(D_MODEL // tn, seq // tm),
        in_specs=[pl.BlockSpec((tm, D_SSM), lambda j, i: (first_pass(j, i), 0)),
                  pl.BlockSpec((tm, D_SSM), lambda j, i: (first_pass(j, i), Z_GS // D_SSM)),
                  pl.BlockSpec((tm, D_ATT), lambda j, i: (i, 0)),
                  pl.BlockSpec((None, D_SSM, D_SSM), lambda j, i: (l, 0, 0), pipeline_mode=pl.Buffered(1)),
                  pl.BlockSpec((None, 1, D_SSM), lambda j, i: (l, 0, 0)),
                  pl.BlockSpec((None, None, D_SSM, tn), lambda j, i: (l, 0, 0, j)),
                  pl.BlockSpec((None, None, D_ATT, tn), lambda j, i: (l, 1, 0, j)),
                  pl.BlockSpec((None, tm, tn), lambda j, i: (kx, i, j)),
                  pl.BlockSpec((None, None, 1, tn), lambda j, i: (l, kb, 0, gate_blk + j))],
        out_specs=pl.BlockSpec((tm, tn), lambda j, i: (i, j)),
        out_shape=jax.ShapeDtypeStruct((seq, D_MODEL), F32),
        scratch_shapes=[pltpu.VMEM((seq // tm, tm, D_SSM), BF16)],
        compiler_params=_params("arbitrary", "arbitrary"),
        name="mix_out",
    )(yg, z, oa, w_glu_b, b_glu3, w_out_b4, w_out_b4, x3, mod4)


def _final_norm_body(x_ref, g_ref, o_ref):
    x = x_ref[...]
    y = x * lax.rsqrt(jnp.mean(x * x, axis=-1, keepdims=True) + EPS)
    o_ref[...] = y * g_ref[...]


def _final_norm(x2, final_g2, tm):
    m = x2.shape[0]
    return pl.pallas_call(
        _final_norm_body,
        grid=(m // tm,),
        in_specs=[pl.BlockSpec((tm, D_MODEL), lambda i: (i, 0)),
                  pl.BlockSpec((1, D_MODEL), lambda i: (0, 0))],
        out_specs=pl.BlockSpec((tm, D_MODEL), lambda i: (i, 0)),
        out_shape=jax.ShapeDtypeStruct((m, D_MODEL), F32),
        compiler_params=_params("arbitrary"),
        name="final_norm",
    )(x2, final_g2)


def _final_norm_seqs_body(*refs):
    *x_refs, g_ref, o_ref = refs
    b = pl.program_id(0)
    x = x_refs[-1][...]
    for k in range(len(x_refs) - 2, -1, -1):
        x = jnp.where(b == k, x_refs[k][...], x)
    o_ref[...] = x * lax.rsqrt(jnp.mean(x * x, axis=-1, keepdims=True) + EPS) * g_ref[...]


def _final_norm_seqs(xs, final_g2, tm):
    seq = xs[0].shape[0]
    last = seq // tm - 1

    def xspec(k):
        return pl.BlockSpec((tm, D_MODEL), lambda b, i: (jnp.where(b == k, i, jnp.where(b > k, last, 0)), 0))

    return pl.pallas_call(
        _final_norm_seqs_body,
        grid=(len(xs), seq // tm),
        in_specs=[xspec(k) for k in range(len(xs))] + [pl.BlockSpec((1, D_MODEL), lambda b, i: (0, 0))],
        out_specs=pl.BlockSpec((None, tm, D_MODEL), lambda b, i: (b, i, 0)),
        out_shape=jax.ShapeDtypeStruct((len(xs), seq, D_MODEL), F32),
        compiler_params=_params("arbitrary", "arbitrary"),
        name="final_norm_seqs",
    )(*xs, final_g2)


def _rope_tables(pos):
    half = HEAD_DIM // 2
    inv = ROPE_THETA ** (-jnp.arange(half, dtype=F32) / half)
    ang = pos.astype(F32)[:, None] * inv[None, :]
    cos, sin = jnp.cos(ang), jnp.sin(ang)
    return jnp.concatenate([cos, cos], axis=-1), jnp.concatenate([-sin, sin], axis=-1)


def kernel(x_prompt, x_sample, c_prompt, c_sample, cache_win_k, cache_win_v, state_ssm_re, state_ssm_im, w_ada, b_ada, norm_g, w_in, ssm_a_re, ssm_a_im, ssm_log_dt, ssm_b_re, ssm_b_im, ssm_c_re, ssm_c_im, ssm_d, w_glu, b_glu, attn_sinks, w_out, final_g):
    bp, seq, _ = x_prompt.shape
    bs = x_sample.shape[0]
    past_len = 8192
    assert seq % (CHUNK * 128) == 0 and x_sample.shape[1] == 1 and cache_win_k.shape[2] == WINDOW
    assert bp + bs <= MOD_ROWS

    w_in_b = w_in.astype(BF16)
    w_glu_b = w_glu.astype(BF16)
    w_out_b4 = w_out.astype(BF16).reshape(DEPTH, 2, D_SSM, D_MODEL)
    norm_g3 = norm_g.reshape(DEPTH, 1, D_MODEL)
    b_glu3 = b_glu.reshape(DEPTH, 1, D_SSM)
    ssm_d3 = ssm_d.reshape(DEPTH, 1, D_SSM)

    c_all = jnp.concatenate([c_prompt, c_sample, jnp.zeros((MOD_ROWS - bp - bs, D_MODEL), F32)], axis=0)
    mod = _modulation(c_all, w_ada, b_ada)
    mod_p = mod[:, :bp].reshape(DEPTH, bp, 1, 3 * D_MODEL)
    mod_s = mod[:, bp:bp + bs].reshape(DEPTH, 1, bs, 3 * D_MODEL)

    prep = _ssm_prep(ssm_a_re, ssm_a_im, ssm_log_dt, ssm_b_re, ssm_b_im, ssm_c_re, ssm_c_im)
    cos_p, sin_p = _rope_tables(jnp.arange(seq))
    cos_s, sin_s = _rope_tables(jnp.full((bs,), past_len))

    xcur = [(x_prompt, k) for k in range(bp)]
    pk = [[None] * bp for _ in range(DEPTH)]
    pv = [[None] * bp for _ in range(DEPTH)]
    pr = [[None] * bp for _ in range(DEPTH)]
    pi = [[None] * bp for _ in range(DEPTH)]

    def finish(z, l, k, sa):
        yg, hr, hi, oa = sa
        x_new = _mix_out(yg, z, oa, w_glu_b, b_glu3, w_out_b4, xcur[k][0], xcur[k][1], mod_p, k, l)
        xcur[k] = (x_new[None], 0)
        pk[l][k] = z[seq - WINDOW:, Z_K:Z_K + D_KV].astype(F32).reshape(WINDOW, N_KV_HEADS, HEAD_DIM)
        pv[l][k] = z[seq - WINDOW:, Z_V:Z_V + D_KV].astype(F32).reshape(WINDOW, N_KV_HEADS, HEAD_DIM)
        pr[l][k] = hr.reshape(N_GROUPS, SSM_STATE)
        pi[l][k] = hi.reshape(N_GROUPS, SSM_STATE)

    pending = None
    for l in range(DEPTH):
        for k in range(bp):
            h = _rms_mod(xcur[k][0], xcur[k][1], norm_g3, mod_p, k, l, 256)
            if pending is None:
                z, = _prompt_stage(h, l, w_in_b, cos_p, sin_p, None, None, prep, ssm_d3, attn_sinks)
            else:
                z, *sa = _prompt_stage(h, l, w_in_b, cos_p, sin_p, pending[0], pending[1], prep, ssm_d3,
                                       attn_sinks)
                finish(*pending, sa)
            pending = (z, l, k)
    sa = _prompt_stage(None, None, w_in_b, cos_p, sin_p, pending[0], pending[1], prep, ssm_d3, attn_sinks)
    finish(*pending, sa)
    y_prompt = _final_norm_seqs([x[0] for x, _ in xcur], final_g.reshape(1, D_MODEL), 256)
    pk, pv, pr, pi = (jnp.stack([jnp.stack(row) for row in t]) for t in (pk, pv, pr, pi))

    xs = x_sample.reshape(bs, D_MODEL)
    sk, sv, sr, si = [], [], [], []
    for l in range(DEPTH):
        h = _rms_mod(xs[None], 0, norm_g3, mod_s, 0, l, bs)
        u, zb = _in_proj(h, w_in_b, cos_s, sin_s, l, bs)
        ut = u.reshape(bs, N_GROUPS, SSM_GROUP).transpose(1, 2, 0)
        ht_r = state_ssm_re[l].transpose(1, 2, 0)
        ht_i = state_ssm_im[l].transpose(1, 2, 0)
        ygt, nhr, nhi = _ssm_step(ut, ht_r, ht_i, prep, ssm_b_re, ssm_b_im, ssm_c_re, ssm_c_im, ssm_d, l)
        yg = ygt.transpose(2, 0, 1).reshape(bs, D_SSM).astype(BF16)
        zf = zb.astype(F32)
        q3 = zf[:, ZB_Q:ZB_Q + D_ATT].reshape(bs, N_HEADS, HEAD_DIM)
        ga3 = zf[:, ZB_GA:ZB_GA + D_ATT].reshape(bs, N_HEADS, HEAD_DIM)
        kn = zf[:, ZB_K:ZB_K + D_KV].reshape(bs, N_KV_HEADS, HEAD_DIM)
        vn = zf[:, ZB_V:ZB_V + D_KV].reshape(bs, N_KV_HEADS, HEAD_DIM)
        oa = _attention_step(q3, ga3, kn, vn, cache_win_k, cache_win_v, attn_sinks, l)
        ys = _glu(yg, zb, ZB_GS, w_glu_b, b_glu3, l, bs)
        xs = _out_proj(ys, oa.reshape(bs, D_ATT).astype(BF16), w_out_b4, xs[None], 0, mod_s, 0, l, bs)
        sk.append(kn.reshape(bs, 1, N_KV_HEADS, HEAD_DIM))
        sv.append(vn.reshape(bs, 1, N_KV_HEADS, HEAD_DIM))
        sr.append(nhr.transpose(2, 0, 1))
        si.append(nhi.transpose(2, 0, 1))
    y_sample = _final_norm(xs, final_g.reshape(1, D_MODEL), bs).reshape(bs, 1, D_MODEL)
    new_k, new_v = _roll_cache(cache_win_k, cache_win_v, jnp.stack(sk), jnp.stack(sv))

    return (y_prompt, y_sample, jnp.stack(pk), jnp.stack(pv), jnp.stack(pr), jnp.stack(pi),
            new_k, new_v, jnp.stack(sr), jnp.stack(si))
```
